```python
import math
import jax, jax.numpy as jnp
from jax import lax
import numpy as np

D_MODEL = 1024
BATCH = 4
SEQ = 4096
DEPTH = 4
DEC_BATCH = 32
DEC_SEQ = 8
PAST_LEN = 8192
PAGE_SIZE = 128

N_MIXERS = 3
HEAD_DIM = 64
W_TOK = 3 * D_MODEL // 4
W_MEM = D_MODEL // 4
H_A = W_TOK // HEAD_DIM
H_MEM = W_MEM // HEAD_DIM
H_IDX = 8
D_IDX = 64
TOPK_MAX = 256
Q_BLOCK = 128
N_BUCKETS = 32
MAX_DISTANCE = 128
CONV_W = 3
SSM_GROUP = 16
SSM_G = W_TOK // SSM_GROUP
SSM_P = 64
D_FF = 4 * D_MODEL
N_MEM = 256
N_A = (DEPTH + N_MIXERS - 1) // N_MIXERS
N_B = (DEPTH + N_MIXERS - 2) // N_MIXERS
N_C = (DEPTH + N_MIXERS - 3) // N_MIXERS
EPS = 1e-6
ATTN_SCALE = HEAD_DIM ** -0.5
IDX_SCALE = (H_IDX * D_IDX) ** -0.5
SPLIT_A = (W_TOK, W_TOK, W_TOK, H_IDX * D_IDX, D_IDX, H_IDX, W_MEM)
SPLIT_B = (W_TOK, W_TOK, W_TOK, W_MEM)
SPLIT_C = (W_TOK, W_MEM)

kernel_name = 'hybrid_dsa_conv_s5_memory_decoder_step'


def _rmsnorm(x, g):
    xf = x.astype(jnp.float32)
    y = xf * lax.rsqrt(jnp.mean(xf * xf, axis=-1, keepdims=True) + EPS)
    return (y * g.astype(jnp.float32)).astype(x.dtype)


def _split(x, sizes):
    return jnp.split(x, np.cumsum(sizes)[:-1].tolist(), axis=-1)


def _rel_bucket(dist):
    n = jnp.maximum(dist, 0)
    exact = N_BUCKETS // 2
    nf = jnp.maximum(n, 1).astype(jnp.float32)
    far = exact + (jnp.log(nf / exact) / math.log(MAX_DISTANCE / exact) * (N_BUCKETS - exact)).astype(jnp.int32)
    return jnp.where(n < exact, n, jnp.minimum(far, N_BUCKETS - 1))


def _dsa_core(q, qi, wi, kidx, qpos, gather_kv, n_keep, rel_bias):
    dots = jnp.einsum('thd,sd->ths', qi.astype(jnp.float32), kidx.astype(jnp.float32))
    score = jnp.einsum('th,ths->ts', wi.astype(jnp.float32) * IDX_SCALE, jax.nn.relu(dots))
    kpos = jnp.arange(kidx.shape[0], dtype=jnp.int32)
    score = jnp.where(kpos[None, :] <= qpos[:, None], score, -jnp.inf)
    _, sel = lax.top_k(score, n_keep)
    ksel, vsel = gather_kv(sel)
    logits = jnp.einsum('thd,tkhd->thk', q.astype(jnp.float32), ksel.astype(jnp.float32)) * ATTN_SCALE
    dist = qpos[:, None] - sel
    bias = rel_bias.astype(jnp.float32)[_rel_bucket(dist)]
    logits = logits + jnp.transpose(bias, (0, 2, 1))
    logits = jnp.where((dist >= 0)[:, None, :], logits, -jnp.inf)
    p = jax.nn.softmax(logits, axis=-1)
    return jnp.einsum('thk,tkhd->thd', p.astype(vsel.dtype), vsel)


def _dsa_prompt(q, k, v, qi, ki, wi, rel_bias):
    bn, s = q.shape[:2]
    nb = s // Q_BLOCK
    n_keep = min(TOPK_MAX, s // 4)

    def blocks(a):
        return a.reshape((bn * nb, Q_BLOCK) + a.shape[2:])

    b_id = jnp.repeat(jnp.arange(bn, dtype=jnp.int32), nb)
    start = jnp.tile(jnp.arange(nb, dtype=jnp.int32) * Q_BLOCK, bn)

    def one(args):
        qb, qib, wib, b, s0 = args
        qpos = s0 + jnp.arange(Q_BLOCK, dtype=jnp.int32)

        def gather(sel):
            return k[b, sel], v[b, sel]

        return _dsa_core(qb, qib, wib, ki[b], qpos, gather, n_keep, rel_bias)

    out = lax.map(one, (blocks(q), blocks(qi), blocks(wi), b_id, start))
    return out.reshape(bn, s, H_A, HEAD_DIM)


def _dsa_sample(q, k, v, qi, ki, wi, pool_k, pool_v, pool_ki, li, page_table, rel_bias):
    bd, t = q.shape[:2]
    past = page_table.shape[1] * PAGE_SIZE
    n_keep = min(TOPK_MAX, (past + t) // 4)
    ki_all = jnp.concatenate([pool_ki[li, page_table].reshape(bd, past, D_IDX), ki], axis=1)
    qpos = past + jnp.arange(t, dtype=jnp.int32)

    def per_seq(qb, kb, vb, qib, kib, wib, pt):
        def gather(sel):
            in_past = (sel < past)[..., None, None]
            ps = jnp.minimum(sel, past - 1)
            phys = pt[ps // PAGE_SIZE]
            off = ps % PAGE_SIZE
            ns = jnp.clip(sel - past, 0, t - 1)
            ks = jnp.where(in_past, pool_k[li, phys, off], kb[ns])
            vs = jnp.where(in_past, pool_v[li, phys, off], vb[ns])
            return ks, vs

        return _dsa_core(qb, qib, wib, kib, qpos, gather, n_keep, rel_bias)

    return jax.vmap(per_seq)(q, k, v, qi, ki_all, wi, page_table)


def _short_conv(h, gb, gc, conv_w, buf):
    t = h.shape[1]
    u = gc * h
    up = jnp.concatenate([buf.astype(u.dtype), u], axis=1)
    y = conv_w[0] * up[:, 0:t]
    for j in range(1, CONV_W):
        y = y + conv_w[j] * up[:, j:j + t]
    return gb * y, up[:, -(CONV_W - 1):]


def _s5(u, a_re, a_im, log_dt, b_re, b_im, c_re, c_im, d, w_glu, b_glu, h0_re, h0_im):
    bn, t, _ = u.shape
    f32 = jnp.float32
    uf = u.astype(f32).reshape(bn, t, SSM_G, SSM_GROUP)
    lam = lax.complex(a_re.astype(f32), a_im.astype(f32))
    dt = jnp.exp(log_dt.astype(f32))[:, None]
    lam_bar = jnp.exp(lam * dt)
    b_bar = ((lam_bar - 1.0) / lam)[..., None] * lax.complex(b_re.astype(f32), b_im.astype(f32))
    c_mat = lax.complex(c_re.astype(f32), c_im.astype(f32))
    bu = jnp.einsum('gpc,btgc->btgp', b_bar, uf.astype(jnp.complex64))
    h0 = lax.complex(h0_re.astype(f32), h0_im.astype(f32))
    bu = bu.at[:, 0].add(lam_bar * h0)
    a = jnp.broadcast_to(lam_bar, bu.shape)

    def comb(l, r):
        return (r[0] * l[0], r[0] * l[1] + r[1])

    _, hs = lax.associative_scan(comb, (a, bu), axis=1)
    y = jnp.real(jnp.einsum('gcp,btgp->btgc', c_mat, hs)) + d.astype(f32).reshape(SSM_G, SSM_GROUP) * uf
    y = jax.nn.gelu(y.reshape(bn, t, W_TOK))
    y = y * jax.nn.sigmoid(y @ w_glu.astype(f32) + b_glu.astype(f32))
    h_last = hs[:, -1]
    return y.astype(u.dtype), jnp.real(h_last), jnp.imag(h_last)


def _mem_kv(mem, g, w_kv, kn):
    bn = mem.shape[0]
    mk, mv = _split(_rmsnorm(mem, g) @ w_kv, (W_MEM, W_MEM))
    mk = _rmsnorm(mk.reshape(bn, N_MEM, H_MEM, HEAD_DIM), kn)
    return mk, mv.reshape(bn, N_MEM, H_MEM, HEAD_DIM)


def _mem_attend(qm, mk, mv):
    logits = jnp.einsum('bthd,bshd->bhts', qm.astype(jnp.float32), mk.astype(jnp.float32)) * ATTN_SCALE
    p = jax.nn.softmax(logits, axis=-1)
    return jnp.einsum('bhts,bshd->bthd', p.astype(mv.dtype), mv)


def _trunk(x, mem_k, mem_v, attend_a, conv_state, ssm_state, p):
    bn, t, _ = x.shape
    ks, vs, kis, convs, res, ims = [], [], [], [], [], []
    for l in range(DEPTH):
        kind, li = l % N_MIXERS, l // N_MIXERS
        h = _rmsnorm(x, p['norm_mix'][l])
        if kind == 0:
            q, k, v, qi, ki, wi, qm = _split(h @ p['w_in_a'][li], SPLIT_A)
            q = _rmsnorm(q.reshape(bn, t, H_A, HEAD_DIM), p['qn_a'][li])
            k = _rmsnorm(k.reshape(bn, t, H_A, HEAD_DIM), p['kn_a'][li])
            v = v.reshape(bn, t, H_A, HEAD_DIM)
            qi = qi.reshape(bn, t, H_IDX, D_IDX)
            mix = attend_a(li, q, k, v, qi, ki, wi).reshape(bn, t, W_TOK)
            ks.append(k)
            vs.append(v)
            kis.append(ki)
        elif kind == 1:
            u, gb, gc, qm = _split(h @ p['w_in_b'][li], SPLIT_B)
            mix, buf = _short_conv(u, gb, gc, p['conv_w'][li], conv_state(li))
            convs.append(buf)
        else:
            u, qm = _split(h @ p['w_in_c'][li], SPLIT_C)
            h_re, h_im = ssm_state(li)
            mix, h_re, h_im = _s5(u, p['ssm_a_re'][li], p['ssm_a_im'][li], p['ssm_log_dt'][li],
                                  p['ssm_b_re'][li], p['ssm_b_im'][li], p['ssm_c_re'][li], p['ssm_c_im'][li],
                                  p['ssm_d'][li], p['w_glu'][li], p['b_glu'][li], h_re, h_im)
            res.append(h_re)
            ims.append(h_im)
        qm = _rmsnorm(qm.reshape(bn, t, H_MEM, HEAD_DIM), p['qn_mem'][l])
        mo = _mem_attend(qm, mem_k[l], mem_v[l]).reshape(bn, t, W_MEM)
        x = x + jnp.concatenate([mix, mo], axis=-1) @ p['w_out'][l]
        hm = _rmsnorm(x, p['norm_mlp'][l])
        x = x + jnp.square(jax.nn.relu(hm @ p['w_up'][l])) @ p['w_down'][l]
    return x, jnp.stack(ks), jnp.stack(vs), jnp.stack(kis), jnp.stack(convs), jnp.stack(res), jnp.stack(ims)


def setup_inputs(seed: int = 0) -> dict:
    key = jax.random.key(seed)
    keys = iter(jax.random.split(key, 64))

    def nrm(shape, scale=1.0):
        return jax.random.normal(next(keys), shape, jnp.float32) * scale

    def gain(shape):
        return 1.0 + nrm(shape, 0.01)

    n_pages = PAST_LEN // PAGE_SIZE
    n_pool = (5 * DEC_BATCH * n_pages) // 4
    p_a, p_b, p_c = sum(SPLIT_A), sum(SPLIT_B), sum(SPLIT_C)
    x_prompt = nrm((BATCH, SEQ, D_MODEL))
    x_sample = nrm((DEC_BATCH, DEC_SEQ, D_MODEL))
    cache_k = nrm((N_A, n_pool, PAGE_SIZE, H_A, HEAD_DIM))
    cache_v = nrm((N_A, n_pool, PAGE_SIZE, H_A, HEAD_DIM))
    cache_kidx = nrm((N_A, n_pool, PAGE_SIZE, D_IDX))
    state_conv = nrm((N_B, DEC_BATCH, CONV_W - 1, W_TOK))
    state_ssm_re = nrm((N_C, DEC_BATCH, SSM_G, SSM_P), 0.5)
    state_ssm_im = nrm((N_C, DEC_BATCH, SSM_G, SSM_P), 0.5)
    cache_mem_k = nrm((DEPTH, DEC_BATCH, N_MEM, H_MEM, HEAD_DIM))
    cache_mem_v = nrm((DEPTH, DEC_BATCH, N_MEM, H_MEM, HEAD_DIM))
    page_table = jax.random.permutation(next(keys), n_pool)[: DEC_BATCH * n_pages].reshape(DEC_BATCH, n_pages).astype(jnp.int32)
    mem_prompt = nrm((BATCH, N_MEM, D_MODEL))
    return {
        'x_prompt': x_prompt,
        'x_sample': x_sample,
        'cache_k': cache_k,
        'cache_v': cache_v,
        'cache_kidx': cache_kidx,
        'state_conv': state_conv,
        'state_ssm_re': state_ssm_re,
        'state_ssm_im': state_ssm_im,
        'cache_mem_k': cache_mem_k,
        'cache_mem_v': cache_mem_v,
        'page_table': page_table,
        'mem_prompt': mem_prompt,
        'norm_mix': gain((DEPTH, D_MODEL)),
        'norm_mem': gain((DEPTH, D_MODEL)),
        'norm_mlp': gain((DEPTH, D_MODEL)),
        'w_in_a': nrm((N_A, D_MODEL, p_a), D_MODEL ** -0.5),
        'w_in_b': nrm((N_B, D_MODEL, p_b), D_MODEL ** -0.5),
        'w_in_c': nrm((N_C, D_MODEL, p_c), D_MODEL ** -0.5),
        'w_out': nrm((DEPTH, W_TOK + W_MEM, D_MODEL), (W_TOK + W_MEM) ** -0.5),
        'qn_a': gain((N_A, HEAD_DIM)),
        'kn_a': gain((N_A, HEAD_DIM)),
        'rel_bias': nrm((N_BUCKETS, H_A), 0.2),
        'conv_w': nrm((N_B, CONV_W, W_TOK), CONV_W ** -0.5),
        'ssm_a_re': -0.5 + nrm((N_C, SSM_G, SSM_P), 0.01),
        'ssm_a_im': math.pi * jnp.arange(SSM_P, dtype=jnp.float32) + nrm((N_C, SSM_G, SSM_P), 0.01),
        'ssm_log_dt': jax.random.uniform(next(keys), (N_C, SSM_G), jnp.float32, math.log(1e-3), math.log(1e-1)),
        'ssm_b_re': nrm((N_C, SSM_G, SSM_P, SSM_GROUP), (2 * SSM_GROUP) ** -0.5),
        'ssm_b_im': nrm((N_C, SSM_G, SSM_P, SSM_GROUP), (2 * SSM_GROUP) ** -0.5),
        'ssm_c_re': nrm((N_C, SSM_G, SSM_GROUP, SSM_P), (2 * SSM_P) ** -0.5),
        'ssm_c_im': nrm((N_C, SSM_G, SSM_GROUP, SSM_P), (2 * SSM_P) ** -0.5),
        'ssm_d': nrm((N_C, W_TOK), 0.5),
        'w_glu': nrm((N_C, W_TOK, W_TOK), W_TOK ** -0.5),
        'b_glu': nrm((N_C, W_TOK), 0.01),
        'w_mem_kv': nrm((DEPTH, D_MODEL, 2 * W_MEM), D_MODEL ** -0.5),
        'qn_mem': gain((DEPTH, HEAD_DIM)),
        'kn_mem': gain((DEPTH, HEAD_DIM)),
        'w_up': nrm((DEPTH, D_MODEL, D_FF), D_MODEL ** -0.5),
        'w_down': nrm((DEPTH, D_FF, D_MODEL), D_FF ** -0.5),
    }


def reference(x_prompt, x_sample, cache_k, cache_v, cache_kidx, state_conv, state_ssm_re, state_ssm_im,
              cache_mem_k, cache_mem_v, page_table, mem_prompt, norm_mix, norm_mem, norm_mlp,
              w_in_a, w_in_b, w_in_c, w_out, qn_a, kn_a, rel_bias, conv_w, ssm_a_re, ssm_a_im, ssm_log_dt,
              ssm_b_re, ssm_b_im, ssm_c_re, ssm_c_im, ssm_d, w_glu, b_glu, w_mem_kv, qn_mem, kn_mem, w_up, w_down):
    p = {
        'norm_mix': norm_mix, 'norm_mlp': norm_mlp, 'w_in_a': w_in_a, 'w_in_b': w_in_b, 'w_in_c': w_in_c,
        'w_out': w_out, 'qn_a': qn_a, 'kn_a': kn_a, 'conv_w': conv_w, 'ssm_a_re': ssm_a_re, 'ssm_a_im': ssm_a_im,
        'ssm_log_dt': ssm_log_dt, 'ssm_b_re': ssm_b_re, 'ssm_b_im': ssm_b_im, 'ssm_c_re': ssm_c_re,
        'ssm_c_im': ssm_c_im, 'ssm_d': ssm_d, 'w_glu': w_glu, 'b_glu': b_glu, 'qn_mem': qn_mem,
        'w_up': w_up, 'w_down': w_down,
    }
    bp = x_prompt.shape[0]
    mem_kv = [_mem_kv(mem_prompt, norm_mem[l], w_mem_kv[l], kn_mem[l]) for l in range(DEPTH)]
    pmk = jnp.stack([m[0] for m in mem_kv])
    pmv = jnp.stack([m[1] for m in mem_kv])

    def attend_prompt(li, q, k, v, qi, ki, wi):
        return _dsa_prompt(q, k, v, qi, ki, wi, rel_bias)

    def conv_prompt(li):
        return jnp.zeros((bp, CONV_W - 1, W_TOK), x_prompt.dtype)

    def ssm_prompt(li):
        z = jnp.zeros((bp, SSM_G, SSM_P), jnp.float32)
        return z, z

    y_prompt, pk, pv, pki, pconv, pre, pim = _trunk(
        x_prompt, [pmk[l] for l in range(DEPTH)], [pmv[l] for l in range(DEPTH)],
        attend_prompt, conv_prompt, ssm_prompt, p)

    def attend_sample(li, q, k, v, qi, ki, wi):
        return _dsa_sample(q, k, v, qi, ki, wi, cache_k, cache_v, cache_kidx, li, page_table, rel_bias)

    def conv_sample(li):
        return state_conv[li]

    def ssm_sample(li):
        return state_ssm_re[li], state_ssm_im[li]

    y_sample, sk, sv, ski, sconv, sre, sim = _trunk(
        x_sample, [cache_mem_k[l] for l in range(DEPTH)], [cache_mem_v[l] for l in range(DEPTH)],
        attend_sample, conv_sample, ssm_sample, p)

    return (y_prompt, y_sample, pk, pv, pki, pconv, pre, pim, pmk, pmv, sk, sv, ski, sconv, sre, sim)
```

```python
import functools
import math

import numpy as np
import jax
import jax.numpy as jnp
from jax import lax
from jax.experimental import pallas as pl
from jax.experimental.pallas import tpu as pltpu

D_MODEL = 1024
HEAD_DIM = 64
W_TOK = 768
W_MEM = 256
H_A = 12
H_MEM = 4
H_IDX = 8
D_IDX = 64
TOPK_MAX = 256
N_BUCKETS = 32
MAX_DISTANCE = 128
CONV_W = 3
SSM_GROUP = 16
SSM_G = 48
SSM_P = 64
SSM_N = SSM_G * SSM_P
D_FF = 4096
N_MEM = 256
PAGE_SIZE = 128
DEPTH = 4
N_MIXERS = 3
EPS = 1e-6
ATTN_SCALE = HEAD_DIM ** -0.5
IDX_SCALE = (H_IDX * D_IDX) ** -0.5

LANE = 128
QB = 256
VMEM_LIMIT = 56 * 1024 * 1024

BF16 = jnp.bfloat16
F32 = jnp.float32
NEG_INF = float("-inf")
INT_MIN = -2 ** 31
KEY_NEG_INF = int(np.array([0xFF800000], np.uint32).view(np.int32)[0]) ^ 0x7FFFFFFF


def _cparams(sem):
    return pltpu.CompilerParams(dimension_semantics=sem, vmem_limit_bytes=VMEM_LIMIT)


def _const_spec(shape):
    nd = len(shape)
    return pl.BlockSpec(shape, lambda *_: (0,) * nd, pipeline_mode=pl.Buffered(1))


def _dot(a, b):
    return jnp.dot(a, b, preferred_element_type=F32)


def _dot_t(a, b):
    return lax.dot_general(a, b, (((1,), (1,)), ((), ())), preferred_element_type=F32)


def _rms(x, g):
    ms = jnp.mean(x * x, axis=-1, keepdims=True)
    return x * lax.rsqrt(ms + EPS) * g


def _head_rms(x, g_tiled, bd):
    x2 = x * x
    hi = x2.astype(BF16)
    r1 = x2 - hi.astype(F32)
    mid = r1.astype(BF16)
    lo = (r1 - mid.astype(F32)).astype(BF16)
    ss = _dot(hi, bd) + _dot(mid, bd) + _dot(lo, bd)
    return x * lax.rsqrt(ss * (1.0 / HEAD_DIM) + EPS) * g_tiled


def _half_masks(rows):
    lane = lax.broadcasted_iota(jnp.int32, (rows, LANE), 1)
    return lane < HEAD_DIM, lane >= HEAD_DIM


A_Q, A_K, A_V, A_QI, A_QM, A_KI, A_WI, A_END = 0, 768, 1536, 2304, 2816, 3072, 3136, 3200


def _in_a_kernel(x_ref, g_ref, w_ref, bd_ref, qn_ref, kn_ref, mn_ref,
                 k_ref, v_ref, ki_ref, qb_ref, kb_ref, vb_ref, qib_ref, ki2_ref, wi_ref, qm_ref):
    h = _rms(x_ref[0], g_ref[...]).astype(BF16)
    proj = _dot(h, w_ref[...])
    bd = bd_ref[...]
    q = _head_rms(proj[:, A_Q:A_K], qn_ref[...], bd) * ATTN_SCALE
    k = _head_rms(proj[:, A_K:A_V], kn_ref[...], bd)
    v = proj[:, A_V:A_QI]
    ki = proj[:, A_KI:A_KI + D_IDX]
    qm = _head_rms(proj[:, A_QM:A_KI], mn_ref[...], bd[:W_MEM, :W_MEM]) * ATTN_SCALE
    k_ref[0] = k
    v_ref[0] = v
    ki_ref[0] = ki
    qb_ref[0] = q.astype(BF16)
    kb_ref[0] = k.astype(BF16)
    vb_ref[0] = v.astype(BF16)
    qib_ref[0] = proj[:, A_QI:A_QM].astype(BF16)
    ki2_ref[0] = jnp.concatenate([ki, ki], axis=-1).astype(BF16)
    wi_ref[0] = proj[:, A_WI:A_WI + H_IDX] * IDX_SCALE
    qm_ref[0] = qm.astype(BF16)


def _in_a(x, g, w, bd, qn, kn, mn, tm):
    b, t, _ = x.shape
    tok = lambda w_, dt: jax.ShapeDtypeStruct((b, t, w_), dt)
    tspec = lambda w_: pl.BlockSpec((1, tm, w_), lambda i, j: (i, j, 0))
    widths = [(W_TOK, F32), (W_TOK, F32), (D_IDX, F32), (W_TOK, BF16), (W_TOK, BF16), (W_TOK, BF16),
              (H_IDX * D_IDX, BF16), (2 * D_IDX, BF16), (H_IDX, F32), (W_MEM, BF16)]
    return pl.pallas_call(
        _in_a_kernel,
        grid=(b, t // tm),
        in_specs=[tspec(D_MODEL), _const_spec((1, D_MODEL)), _const_spec((D_MODEL, A_END)),
                  _const_spec((W_TOK, W_TOK)), _const_spec((1, W_TOK)), _const_spec((1, W_TOK)),
                  _const_spec((1, W_MEM))],
        out_specs=[tspec(w_) for w_, _ in widths],
        out_shape=[tok(w_, dt) for w_, dt in widths],
        compiler_params=_cparams(("parallel", "parallel")),
        name="in_proj_dsa",
    )(x, g, w, bd, qn, kn, mn)


def _in_b_kernel(x_ref, g_ref, w_ref, bd_ref, mn_ref, cw_ref, st_ref,
                 mix_ref, qm_ref, nst_ref, ubuf):
    j = pl.program_id(1)
    tm = x_ref.shape[1]

    @pl.when(j == 0)
    def _():
        ubuf[6:8, :] = st_ref[0]

    h = _rms(x_ref[0], g_ref[...]).astype(BF16)
    proj = _dot(h, w_ref[...])
    u = proj[:, 0:W_TOK]
    gb = proj[:, W_TOK:2 * W_TOK]
    gc = proj[:, 2 * W_TOK:3 * W_TOK]
    ubuf[8:8 + tm, :] = gc * u
    y = cw_ref[0:1, :] * ubuf[6:6 + tm, :]
    y = y + cw_ref[1:2, :] * ubuf[7:7 + tm, :]
    y = y + cw_ref[2:3, :] * ubuf[8:8 + tm, :]
    mix_ref[0] = (gb * y).astype(BF16)
    last = ubuf[6 + tm:8 + tm, :]
    nst_ref[0] = last
    ubuf[6:8, :] = last
    qm = _head_rms(proj[:, 3 * W_TOK:], mn_ref[...], bd_ref[...]) * ATTN_SCALE
    qm_ref[0] = qm.astype(BF16)


def _in_b(x, g, w, bd, mn, cw, state, tm):
    b, t, _ = x.shape
    tspec = lambda w_: pl.BlockSpec((1, tm, w_), lambda i, j: (i, j, 0))
    return pl.pallas_call(
        _in_b_kernel,
        grid=(b, t // tm),
        in_specs=[tspec(D_MODEL), _const_spec((1, D_MODEL)), _const_spec((D_MODEL, 3 * W_TOK + W_MEM)),
                  _const_spec((W_MEM, W_MEM)), _const_spec((1, W_MEM)), _const_spec((CONV_W, W_TOK)),
                  pl.BlockSpec((1, CONV_W - 1, W_TOK), lambda i, j: (i, 0, 0))],
        out_specs=[tspec(W_TOK), tspec(W_MEM), pl.BlockSpec((1, CONV_W - 1, W_TOK), lambda i, j: (i, 0, 0))],
        out_shape=[jax.ShapeDtypeStruct((b, t, W_TOK), BF16), jax.ShapeDtypeStruct((b, t, W_MEM), BF16),
                   jax.ShapeDtypeStruct((b, CONV_W - 1, W_TOK), F32)],
        scratch_shapes=[pltpu.VMEM((tm + 8, W_TOK), F32)],
        compiler_params=_cparams(("parallel", "arbitrary")),
        name="in_proj_conv",
    )(x, g, w, bd, mn, cw, state)


def _s5_disc_kernel(are_ref, aim_ref, ldt_ref, bre_ref, bim_ref, lam_ref, bbre_ref, bbim_ref):
    a_re = are_ref[...]
    a_im = aim_ref[...]
    dt = jnp.exp(ldt_ref[...])
    mag = jnp.exp(a_re * dt)
    l_re = mag * jnp.cos(a_im * dt)
    l_im = mag * jnp.sin(a_im * dt)
    x_re = l_re - 1.0
    den = a_re * a_re + a_im * a_im
    c_re = (x_re * a_re + l_im * a_im) / den
    c_im = (l_im * a_re - x_re * a_im) / den
    lam_ref[0] = l_re
    lam_ref[1] = l_im
    b_re = bre_ref[...]
    b_im = bim_ref[...]
    bbre_ref[...] = c_re[:, None, :] * b_re - c_im[:, None, :] * b_im
    bbim_ref[...] = c_re[:, None, :] * b_im + c_im[:, None, :] * b_re


def _s5_disc(a_re, a_im, log_dt, b_re_t, b_im_t):
    g3 = jax.ShapeDtypeStruct((SSM_G, SSM_GROUP, SSM_P), F32)
    return pl.pallas_call(
        _s5_disc_kernel,
        out_shape=[jax.ShapeDtypeStruct((2, SSM_G, SSM_P), F32), g3, g3],
        name="s5_discretise",
    )(a_re, a_im, log_dt.reshape(SSM_G, 1), b_re_t, b_im_t)


S5_LANES = 512
N_BT = SSM_N // QB
N_CT = W_TOK // QB


def _s5_kernel(x_ref, g_ref, w_ref, bd_ref, mn_ref, bwre_ref, bwim_ref, lam_ref, cwre_ref, cwim_ref,
               d_ref, wglu_ref, bglu_ref, h0_ref,
               mix_ref, qm_ref, hout_ref, sre, sim):
    j = pl.program_id(1)
    tt = x_ref.shape[1]

    @pl.when(j == 0)
    def _():
        hout_ref[0] = h0_ref[0]

    h = _rms(x_ref[0], g_ref[...]).astype(BF16)
    proj = _dot(h, w_ref[...])
    u = proj[:, :W_TOK]
    ub = u.astype(BF16)
    for jt in range(N_BT):
        uk = ub[:, LANE * (jt // 2):LANE * (jt // 2 + 1)]
        sre[:, QB * jt:QB * (jt + 1)] = _dot(uk, bwre_ref[jt])
        sim[:, QB * jt:QB * (jt + 1)] = _dot(uk, bwim_ref[jt])

    for lc in range(SSM_N // S5_LANES):
        sl = slice(lc * S5_LANES, (lc + 1) * S5_LANES)
        l_re = lam_ref[0:1, sl]
        l_im = lam_ref[1:2, sl]

        def step(i, carry, sl=sl, l_re=l_re, l_im=l_im):
            h_re, h_im = carry
            n_re = l_re * h_re - l_im * h_im + sre[pl.ds(i, 1), sl]
            n_im = l_re * h_im + l_im * h_re + sim[pl.ds(i, 1), sl]
            sre[pl.ds(i, 1), sl] = n_re
            sim[pl.ds(i, 1), sl] = n_im
            return n_re, n_im

        h_re, h_im = lax.fori_loop(0, tt, step, (hout_ref[0, 0:1, sl], hout_ref[0, 1:2, sl]))
        hout_ref[0, 0:1, sl] = h_re
        hout_ref[0, 1:2, sl] = h_im

    ys = []
    for jc in range(N_CT):
        ks = slice(jc * 4 * QB, (jc + 1) * 4 * QB)
        ys.append(_dot(sre[:, ks].astype(BF16), cwre_ref[jc]) - _dot(sim[:, ks].astype(BF16), cwim_ref[jc]))
    y = jnp.concatenate(ys, axis=-1) + d_ref[...] * u
    y = jax.nn.gelu(y)
    z = _dot(y.astype(BF16), wglu_ref[...]) + bglu_ref[...]
    mix_ref[0] = (y * (1.0 / (1.0 + jnp.exp(-z)))).astype(BF16)
    qm = _head_rms(proj[:, W_TOK:], mn_ref[...], bd_ref[...]) * ATTN_SCALE
    qm_ref[0] = qm.astype(BF16)


def _s5(x, g, w, bd, mn, bwre, bwim, lam, cwre, cwim, d, wglu, bglu, h0, tt):
    b, t, _ = x.shape
    tspec = lambda w_: pl.BlockSpec((1, tt, w_), lambda i, j: (i, j, 0))
    sspec = pl.BlockSpec((1, 2, SSM_N), lambda i, j: (i, 0, 0))
    return pl.pallas_call(
        _s5_kernel,
        grid=(b, t // tt),
        in_specs=[tspec(D_MODEL), _const_spec((1, D_MODEL)), _const_spec((D_MODEL, W_TOK + W_MEM)),
                  _const_spec((W_MEM, W_MEM)), _const_spec((1, W_MEM)),
                  _const_spec((N_BT, LANE, QB)), _const_spec((N_BT, LANE, QB)), _const_spec((2, SSM_N)),
                  _const_spec((N_CT, 4 * QB, QB)), _const_spec((N_CT, 4 * QB, QB)),
                  _const_spec((1, W_TOK)), _const_spec((W_TOK, W_TOK)), _const_spec((1, W_TOK)), sspec],
        out_specs=[tspec(W_TOK), tspec(W_MEM), sspec],
        out_shape=[jax.ShapeDtypeStruct((b, t, W_TOK), BF16), jax.ShapeDtypeStruct((b, t, W_MEM), BF16),
                   jax.ShapeDtypeStruct((b, 2, SSM_N), F32)],
        scratch_shapes=[pltpu.VMEM((tt, SSM_N), F32), pltpu.VMEM((tt, SSM_N), F32)],
        compiler_params=_cparams(("parallel", "arbitrary")),
        name="s5_mixer",
    )(x, g, w, bd, mn, bwre, bwim, lam, cwre, cwim, d, wglu, bglu, h0)


def _s5_weights(b_bar_re, b_bar_im, c_re, c_im):
    eye = jnp.eye(SSM_G, dtype=F32)

    def b_tiles(bb):
        full = jnp.einsum("gcp,gh->gchp", bb, eye).reshape(W_TOK, SSM_N)
        return jnp.stack([full[LANE * (j // 2):LANE * (j // 2 + 1), QB * j:QB * (j + 1)]
                          for j in range(N_BT)]).astype(BF16)

    def c_tiles(cc):
        full = jnp.einsum("gcp,gh->gphc", cc, eye).reshape(SSM_N, W_TOK)
        return jnp.stack([full[4 * QB * j:4 * QB * (j + 1), QB * j:QB * (j + 1)]
                          for j in range(N_CT)]).astype(BF16)

    return b_tiles(b_bar_re), b_tiles(b_bar_im), c_tiles(c_re), c_tiles(c_im)


def _mem_kv_kernel(m_ref, g_ref, w_ref, bd_ref, kn_ref, k_ref, v_ref, kb_ref, vb_ref):
    h = _rms(m_ref[0], g_ref[0]).astype(BF16)
    kv = _dot(h, w_ref[0])
    k = _head_rms(kv[:, :W_MEM], kn_ref[0], bd_ref[...]).T
    v = kv[:, W_MEM:].T
    k_ref[0, 0] = k
    v_ref[0, 0] = v
    kb_ref[0, 0] = k.astype(BF16)
    vb_ref[0, 0] = v.astype(BF16)


def _mem_kv(mem, g, w, bd, kn):
    b = mem.shape[0]
    ospec = pl.BlockSpec((1, 1, N_MEM, W_MEM), lambda l, i: (l, i, 0, 0))
    osh = lambda dt: jax.ShapeDtypeStruct((DEPTH, b, N_MEM, W_MEM), dt)
    return pl.pallas_call(
        _mem_kv_kernel,
        grid=(DEPTH, b),
        in_specs=[pl.BlockSpec((1, N_MEM, D_MODEL), lambda l, i: (i, 0, 0)),
                  pl.BlockSpec((1, 1, D_MODEL), lambda l, i: (l, 0, 0)),
                  pl.BlockSpec((1, D_MODEL, 2 * W_MEM), lambda l, i: (l, 0, 0)),
                  _const_spec((W_MEM, W_MEM)),
                  pl.BlockSpec((1, 1, W_MEM), lambda l, i: (l, 0, 0))],
        out_specs=[ospec] * 4,
        out_shape=[osh(F32), osh(F32), osh(BF16), osh(BF16)],
        compiler_params=_cparams(("parallel", "parallel")),
        name="mem_kv",
    )(mem, g, w, bd, kn)


def _mem_attn_kernel(qm_ref, k_ref, v_ref, o_ref):
    tm = qm_ref.shape[1]
    lo, hi = _half_masks(tm)
    outs = []
    for p in range(H_MEM // 2):
        ls = slice(p * LANE, (p + 1) * LANE)
        qp = qm_ref[0, :, ls]
        kp = k_ref[0, ls, :]
        vp = v_ref[0, ls, :]
        halves = []
        for msk in (lo, hi):
            s = _dot(jnp.where(msk, qp, jnp.zeros_like(qp)), kp)
            e = jnp.exp(s - jnp.max(s, axis=-1, keepdims=True))
            o = _dot_t(e.astype(BF16), vp)
            halves.append(o / jnp.sum(e, axis=-1, keepdims=True))
        outs.append(jnp.where(lo, halves[0], halves[1]))
    o_ref[0] = jnp.concatenate(outs, axis=-1).astype(BF16)


def _mem_attn(qm, k, v, tm):
    b, t, _ = qm.shape
    kvspec = pl.BlockSpec((1, N_MEM, W_MEM), lambda i, j: (i, 0, 0))
    tspec = pl.BlockSpec((1, tm, W_MEM), lambda i, j: (i, j, 0))
    return pl.pallas_call(
        _mem_attn_kernel,
        grid=(b, t // tm),
        in_specs=[tspec, kvspec, kvspec],
        out_specs=tspec,
        out_shape=jax.ShapeDtypeStruct((b, t, W_MEM), BF16),
        compiler_params=_cparams(("parallel", "parallel")),
        name="mem_attn",
    )(qm, k, v)


FF_CHUNK = 1024


def _post_kernel(x_ref, mix_ref, mo_ref, wo_ref, g_ref, wu_ref, wd_ref, o_ref):
    o_ref[0] = x_ref[0] + _dot(mix_ref[0], wo_ref[:W_TOK, :]) + _dot(mo_ref[0], wo_ref[W_TOK:, :])
    hm = _rms(o_ref[0], g_ref[...]).astype(BF16)
    for f in range(D_FF // FF_CHUNK):
        fs = slice(f * FF_CHUNK, (f + 1) * FF_CHUNK)
        a = jnp.maximum(_dot(hm, wu_ref[:, fs]), 0.0)
        o_ref[0] += _dot((a * a).astype(BF16), wd_ref[fs, :])


def _post(x, mix, mo, wo, g, wu, wd, tm):
    b, t, _ = x.shape
    tspec = lambda w_: pl.BlockSpec((1, tm, w_), lambda i, j: (i, j, 0))
    return pl.pallas_call(
        _post_kernel,
        grid=(b, t // tm),
        in_specs=[tspec(D_MODEL), tspec(W_TOK), tspec(W_MEM), _const_spec((D_MODEL, D_MODEL)),
                  _const_spec((1, D_MODEL)), _const_spec((D_MODEL, D_FF)), _const_spec((D_FF, D_MODEL))],
        out_specs=tspec(D_MODEL),
        out_shape=jax.ShapeDtypeStruct((b, t, D_MODEL), F32),
        compiler_params=_cparams(("parallel", "parallel")),
        name="out_proj_mlp",
    )(x, mix, mo, wo, g, wu, wd)


def _bucket_table(n):
    d = np.arange(n)
    exact = N_BUCKETS // 2
    nf = np.maximum(d, 1).astype(np.float32)
    far = exact + (np.log(nf / np.float32(exact)) / np.float32(math.log(MAX_DISTANCE / exact))
                   * np.float32(N_BUCKETS - exact)).astype(np.int32)
    return np.where(d < exact, d, np.minimum(far, N_BUCKETS - 1)).astype(np.int32)


def _bias_kernel(rb_ref, bk_ref, o_ref):
    bk = bk_ref[...]
    for h in range(H_A):
        far = rb_ref[N_BUCKETS - 1, h]
        for i in range(bk.shape[0]):
            t = jnp.zeros(bk.shape[1:], F32)
            for b in range(N_BUCKETS - 1):
                t = jnp.where(bk[i] == b, rb_ref[b, h] - far, t)
            o_ref[h, i] = t


def _bias_tiles(rel_bias, buckets):
    return pl.pallas_call(
        _bias_kernel,
        in_specs=[pl.BlockSpec(memory_space=pltpu.SMEM), pl.BlockSpec(memory_space=pltpu.VMEM)],
        out_shape=jax.ShapeDtypeStruct((H_A,) + buckets.shape, F32),
        compiler_params=pltpu.CompilerParams(vmem_limit_bytes=VMEM_LIMIT),
        name="rel_bias_tiles",
    )(rel_bias, buckets)


def _prompt_buckets():
    table = _bucket_table(2 * QB)
    tq = np.arange(QB)[:, None]
    sk = np.arange(QB)[None, :]
    diag = table[np.maximum(tq - sk, 0)]
    off = table[QB + tq - sk]
    return np.stack([off, diag]).astype(np.int32)


def _sort_key(s):
    s = jnp.where(s == 0.0, 0.0, s)
    bits = pltpu.bitcast(s, jnp.int32)
    return jnp.where(bits < 0, bits ^ jnp.int32(0x7FFFFFFF), bits)


def _dsa_prompt_kernel(n_keep, qi_ref, wit_ref, ki2_ref, q_ref, k_ref, v_ref, bias_ref, o_ref,
                       kbuf, mbuf, lbuf, mx_ref, l_ref, acc_ref):
    j = pl.program_id(1)
    nc = j + 1
    lo, hi = _half_masks(QB)
    zero_b = jnp.zeros((QB, LANE), BF16)

    qi = qi_ref[0]
    qim = [jnp.where(hi if h % 2 else lo, qi[:, LANE * (h // 2):LANE * (h // 2 + 1)], zero_b)
           for h in range(H_IDX)]
    qpos = j * QB + lax.broadcasted_iota(jnp.int32, (QB, QB), 1)
    krow = lax.broadcasted_iota(jnp.int32, (QB, QB), 0)

    def score_chunk(c, _):
        kc = ki2_ref[0, pl.ds(pl.multiple_of(c * QB, QB), QB), :]
        acc = jnp.zeros((QB, QB), F32)
        for h in range(H_IDX):
            acc = acc + wit_ref[0, h:h + 1, :] * jnp.maximum(_dot_t(kc, qim[h]), 0.0)
        s = jnp.where(c * QB + krow <= qpos, acc, NEG_INF)
        kbuf[pl.ds(pl.multiple_of(c * QB, QB), QB), :] = _sort_key(s)
        return 0

    lax.fori_loop(0, nc, score_chunk, 0)

    def count(pred):
        def body(r, acc):
            blk = kbuf[pl.ds(pl.multiple_of(r * 64, 64), 64), :]
            idx = r * 64 + lax.broadcasted_iota(jnp.int32, (64, QB), 0)
            hit = jnp.where(pred(blk, idx), 1, 0)
            return acc + jnp.sum(hit.reshape(8, 8, QB), axis=0)
        acc = lax.fori_loop(0, nc * (QB // 64), body, jnp.zeros((8, QB), jnp.int32))
        return jnp.sum(acc, axis=0, keepdims=True)

    def bisect(i, carry):
        t, cnt_t = carry
        cand = t + lax.shift_left(jnp.int32(1), 31 - i)
        cnt = count(lambda blk, idx: blk >= cand)
        ok = cnt >= n_keep
        return jnp.where(ok, cand, t), jnp.where(ok, cnt, cnt_t)

    t, cnt_t = lax.fori_loop(0, 32, bisect, (jnp.full((1, QB), INT_MIN, jnp.int32),
                                             jnp.full((1, QB), 2 ** 30, jnp.int32)))

    tied = jnp.logical_and(cnt_t > n_keep, t > KEY_NEG_INF)
    any_tied = jnp.max(jnp.where(tied, 1, 0)) > 0

    def tie_cut():
        need = n_keep - count(lambda blk, idx: blk > t)

        def step(i, x):
            cand = x + lax.shift_left(jnp.int32(1), 12 - i)
            cnt = count(lambda blk, idx: jnp.logical_and(blk == t, idx < cand))
            return jnp.where(cnt < need, cand, x)

        return lax.fori_loop(0, 13, step, jnp.zeros((1, QB), jnp.int32))

    jcut = lax.cond(any_tied, tie_cut, lambda: jnp.full((1, QB), 2 ** 30, jnp.int32))

    def mask_chunk(c, _):
        key = kbuf[pl.ds(pl.multiple_of(c * QB, QB), QB), :]
        idx = c * QB + krow
        m = jnp.where(key > t, 0.0, jnp.where(key == t, jnp.where(idx <= jcut, 0.0, NEG_INF), NEG_INF))
        m = jnp.where(key > KEY_NEG_INF, m, NEG_INF)
        mbuf[c] = m.T
        return 0

    lax.fori_loop(0, nc, mask_chunk, 0)

    for p in range(H_A // 2):
        ls = slice(p * LANE, (p + 1) * LANE)
        qp = q_ref[0, :, ls]
        outs = []
        for hh, msk in enumerate((lo, hi)):
            h = 2 * p + hh
            qh = jnp.where(msk, qp, zero_b)

            def logits(c, bias, qh=qh, ls=ls):
                s = _dot_t(qh, k_ref[0, pl.ds(pl.multiple_of(c * QB, QB), QB), ls]) + mbuf[c]
                if bias is not None:
                    s = s + bias
                lbuf[c] = s
                mx_ref[...] = jnp.maximum(mx_ref[...], jnp.maximum(s[:, :LANE], s[:, LANE:]))

            mx_ref[...] = jnp.full((QB, LANE), NEG_INF, F32)

            def far(c, _, logits=logits):
                logits(c, None)
                return 0

            lax.fori_loop(0, jnp.maximum(j - 1, 0), far, 0)

            @pl.when(j >= 1)
            def _(logits=logits, h=h):
                logits(j - 1, bias_ref[h, 0])

            logits(j, bias_ref[h, 1])
            m = jnp.max(mx_ref[...], axis=-1, keepdims=True)

            l_ref[...] = jnp.zeros((QB, LANE), F32)
            acc_ref[...] = jnp.zeros((QB, LANE), F32)

            def pv(c, _, m=m, ls=ls):
                e = jnp.exp(lbuf[c] - m)
                l_ref[...] += e[:, :LANE] + e[:, LANE:]
                acc_ref[...] += _dot(e.astype(BF16), v_ref[0, pl.ds(pl.multiple_of(c * QB, QB), QB), ls])
                return 0

            lax.fori_loop(0, nc, pv, 0)
            outs.append(acc_ref[...] / jnp.sum(l_ref[...], axis=-1, keepdims=True))
        o_ref[0, :, ls] = jnp.where(lo, outs[0], outs[1]).astype(BF16)


def _dsa_prompt(qi, wit, ki2, q, k, v, bias, n_keep):
    b, t, _ = q.shape
    nq = t // QB
    qspec = lambda w_: pl.BlockSpec((1, QB, w_), lambda i, j: (i, j, 0))
    kspec = lambda w_: pl.BlockSpec((1, t, w_), lambda i, j: (i, 0, 0))
    return pl.pallas_call(
        functools.partial(_dsa_prompt_kernel, n_keep),
        grid=(b, nq),
        in_specs=[qspec(H_IDX * D_IDX), pl.BlockSpec((1, H_IDX, QB), lambda i, j: (i, 0, j)),
                  kspec(2 * D_IDX), qspec(W_TOK), kspec(W_TOK), kspec(W_TOK),
                  _const_spec((H_A, 2, QB, QB))],
        out_specs=qspec(W_TOK),
        out_shape=jax.ShapeDtypeStruct((b, t, W_TOK), BF16),
        scratch_shapes=[pltpu.VMEM((t, QB), jnp.int32), pltpu.VMEM((nq, QB, QB), F32),
                        pltpu.VMEM((nq, QB, QB), F32), pltpu.VMEM((QB, LANE), F32),
                        pltpu.VMEM((QB, LANE), F32), pltpu.VMEM((QB, LANE), F32)],
        compiler_params=_cparams(("parallel", "arbitrary")),
        name="dsa_prompt",
    )(qi, wit, ki2, q, k, v, bias)


def _prep_weights(p):
    seg = np.arange(W_TOK) // HEAD_DIM
    bd = jnp.asarray(seg[:, None] == seg[None, :], BF16)
    row = lambda a: a.reshape(a.shape[0], 1, a.shape[-1])
    tile = lambda a, n: row(jnp.tile(a, (1, n)))
    wa = p["w_in_a"]
    pad = jnp.zeros(wa.shape[:2] + (A_END - A_WI - H_IDX,), wa.dtype)
    wa = jnp.concatenate([wa[..., 0:2304], wa[..., 2304:2816], wa[..., 2888:3144], wa[..., 2816:2880],
                          wa[..., 2880:2888], pad], axis=-1)
    return {
        "bd": bd,
        "norm_mix": row(p["norm_mix"]), "norm_mem": row(p["norm_mem"]), "norm_mlp": row(p["norm_mlp"]),
        "w_in_a": wa.astype(BF16), "w_in_b": p["w_in_b"].astype(BF16), "w_in_c": p["w_in_c"].astype(BF16),
        "w_out": p["w_out"].astype(BF16), "w_up": p["w_up"].astype(BF16), "w_down": p["w_down"].astype(BF16),
        "w_mem_kv": p["w_mem_kv"].astype(BF16),
        "qn_a": tile(p["qn_a"], H_A), "kn_a": tile(p["kn_a"], H_A),
        "qn_mem": tile(p["qn_mem"], H_MEM), "kn_mem": tile(p["kn_mem"], H_MEM),
    }


def _s5_prep(p, li):
    lam, bb_re, bb_im = _s5_disc(p["ssm_a_re"][li], p["ssm_a_im"][li], p["ssm_log_dt"][li],
                                 jnp.swapaxes(p["ssm_b_re"][li], 1, 2), jnp.swapaxes(p["ssm_b_im"][li], 1, 2))
    bwre, bwim, cwre, cwim = _s5_weights(bb_re, bb_im, p["ssm_c_re"][li], p["ssm_c_im"][li])
    return (bwre, bwim, lam.reshape(2, SSM_N), cwre, cwim, p["ssm_d"][li].reshape(1, W_TOK),
            p["w_glu"][li].astype(BF16), p["b_glu"][li].reshape(1, W_TOK))


PG = 8


def _sample_buckets(t_new):
    table = _bucket_table(2 * PAGE_SIZE + t_new)
    t = np.arange(t_new)[:, None]
    lane = np.arange(PAGE_SIZE)[None, :]
    return np.stack([table[PAGE_SIZE + t - lane], table[np.maximum(t - lane, 0)]]).astype(np.int32)


def _dsa_select_kernel(n_keep, n_pages, pt_ref, qis_ref, wi_ref, kin_ref, *rest):
    pages = rest[:PG]
    mask_ref, sbuf = rest[PG], rest[PG + 1]
    g = pl.program_id(1)
    t_new = wi_ref.shape[1]
    qis = qis_ref[0]
    wi = wi_ref[0]

    def scores(d):
        acc = jnp.zeros((t_new, PAGE_SIZE), F32)
        for h in range(H_IDX):
            acc = acc + wi[:, h:h + 1] * jnp.maximum(d[h * t_new:(h + 1) * t_new, :], 0.0)
        return acc

    for r in range(PG):
        sbuf[g * PG + r] = _sort_key(scores(_dot(qis, pages[r][0, 0].astype(BF16))))

    @pl.when(g == pl.num_programs(1) - 1)
    def _():
        trow = lax.broadcasted_iota(jnp.int32, (t_new, PAGE_SIZE), 0)
        lane = lax.broadcasted_iota(jnp.int32, (t_new, PAGE_SIZE), 1)
        sbuf[n_pages] = _sort_key(jnp.where(lane <= trow, scores(_dot_t(qis, kin_ref[0])), NEG_INF))

        def count(pred):
            acc = jnp.zeros((t_new, PAGE_SIZE), jnp.int32)
            for blk in range(n_pages + 1):
                acc = acc + jnp.where(pred(sbuf[blk], blk * PAGE_SIZE + lane), 1, 0)
            return jnp.sum(acc, axis=-1, keepdims=True)

        def bisect(i, carry):
            t, cnt_t = carry
            cand = t + lax.shift_left(jnp.int32(1), 31 - i)
            cnt = count(lambda blk, idx: blk >= cand)
            ok = cnt >= n_keep
            return jnp.where(ok, cand, t), jnp.where(ok, cnt, cnt_t)

        t, cnt_t = lax.fori_loop(0, 32, bisect, (jnp.full((t_new, 1), INT_MIN, jnp.int32),
                                                 jnp.full((t_new, 1), 2 ** 30, jnp.int32)))
        tied = jnp.logical_and(cnt_t > n_keep, t > KEY_NEG_INF)
        any_tied = jnp.max(jnp.where(tied, 1, 0)) > 0

        def tie_cut():
            need = n_keep - count(lambda blk, idx: blk > t)

            def step(i, x):
                cand = x + lax.shift_left(jnp.int32(1), 14 - i)
                cnt = count(lambda blk, idx: jnp.logical_and(blk == t, idx < cand))
                return jnp.where(cnt < need, cand, x)

            return lax.fori_loop(0, 15, step, jnp.zeros((t_new, 1), jnp.int32))

        jcut = lax.cond(any_tied, tie_cut, lambda: jnp.full((t_new, 1), 2 ** 30, jnp.int32))
        for blk in range(n_pages + 1):
            key = sbuf[blk]
            idx = blk * PAGE_SIZE + lane
            m = jnp.where(key > t, 0.0, jnp.where(key == t, jnp.where(idx <= jcut, 0.0, NEG_INF), NEG_INF))
            mask_ref[0, blk] = jnp.where(key > KEY_NEG_INF, m, NEG_INF)


def _dsa_select(page_table, qis, wi, ki_new, pool_ki, li, n_keep):
    bd_, n_pages = page_table.shape
    t_new = wi.shape[1]
    page_spec = lambda r: pl.BlockSpec((1, 1, D_IDX, PAGE_SIZE),
                                       lambda b, g, pt, r=r: (li, pt[b, g * PG + r], 0, 0))
    grid_spec = pltpu.PrefetchScalarGridSpec(
        num_scalar_prefetch=1,
        grid=(bd_, n_pages // PG),
        in_specs=[pl.BlockSpec((1, H_IDX * t_new, D_IDX), lambda b, g, pt: (b, 0, 0)),
                  pl.BlockSpec((1, t_new, H_IDX), lambda b, g, pt: (b, 0, 0)),
                  pl.BlockSpec((1, PAGE_SIZE, D_IDX), lambda b, g, pt: (b, 0, 0))]
                 + [page_spec(r) for r in range(PG)],
        out_specs=pl.BlockSpec((1, n_pages + 1, t_new, PAGE_SIZE), lambda b, g, pt: (b, 0, 0, 0)),
        scratch_shapes=[pltpu.VMEM((n_pages + 1, t_new, PAGE_SIZE), jnp.int32)],
    )
    return pl.pallas_call(
        functools.partial(_dsa_select_kernel, n_keep, n_pages),
        grid_spec=grid_spec,
        out_shape=jax.ShapeDtypeStruct((bd_, n_pages + 1, t_new, PAGE_SIZE), F32),
        compiler_params=_cparams(("parallel", "arbitrary")),
        name="dsa_sample_select",
    )(page_table, qis, wi, ki_new, *([pool_ki] * PG))


def _dsa_attend_kernel(n_pages, pt_ref, qbd_ref, mask_ref, bias_ref, kn_ref, vn_ref, *rest):
    kpages, vpages = rest[:PG], rest[PG:2 * PG]
    o_ref, lbuf, lnew, mx_ref, l_ref, acc_ref, stage = rest[2 * PG:]
    ph = pl.program_id(1)
    g = pl.program_id(2)
    ng = pl.num_programs(2)
    t_new = mask_ref.shape[2]
    rows = H_A * t_new
    qbd = qbd_ref[0]

    def fold(x):
        return [x[:, i * LANE:(i + 1) * LANE] for i in range(x.shape[1] // LANE)]

    def head_tile(m):
        return jnp.tile(m, (H_A, 1))

    @pl.when(ph == 0)
    def _():
        @pl.when(g == 0)
        def _():
            mx_ref[...] = jnp.full((rows, LANE), NEG_INF, F32)

        for r in range(PG):
            stage[:, r * PAGE_SIZE:(r + 1) * PAGE_SIZE] = kpages[r][0, 0].astype(BF16)
        s = _dot(qbd, stage[...])
        parts = []
        for r in range(PG):
            part = s[:, r * LANE:(r + 1) * LANE] + head_tile(mask_ref[0, g * PG + r])
            if r == PG - 1:
                part = part + jnp.where(g == ng - 1, 1.0, 0.0) * bias_ref[0]
            parts.append(part)
        lbuf[g] = jnp.concatenate(parts, axis=-1)
        mx = mx_ref[...]
        for part in parts:
            mx = jnp.maximum(mx, part)
        mx_ref[...] = mx

        @pl.when(g == ng - 1)
        def _():
            sn = _dot_t(qbd, kn_ref[0]) + head_tile(mask_ref[0, n_pages]) + bias_ref[1]
            lnew[...] = sn
            mx_ref[...] = jnp.maximum(mx_ref[...], sn)

    @pl.when(ph == 1)
    def _():
        m = jnp.max(mx_ref[...], axis=-1, keepdims=True)

        @pl.when(g == 0)
        def _():
            en = jnp.exp(lnew[...] - m)
            l_ref[...] = en
            acc_ref[...] = _dot(en.astype(BF16), vn_ref[0])

        for r in range(PG):
            stage[:, r * PAGE_SIZE:(r + 1) * PAGE_SIZE] = vpages[r][0, 0].astype(BF16)
        e = jnp.exp(lbuf[g] - m)
        lsum = l_ref[...]
        for part in fold(e):
            lsum = lsum + part
        l_ref[...] = lsum
        acc_ref[...] += _dot_t(e.astype(BF16), stage[...])

        @pl.when(g == ng - 1)
        def _():
            o = acc_ref[...] / jnp.sum(l_ref[...], axis=-1, keepdims=True)
            lo, _ = _half_masks(t_new)
            outs = []
            for p in range(H_A // 2):
                ls = slice(p * LANE, (p + 1) * LANE)
                outs.append(jnp.where(lo, o[2 * p * t_new:(2 * p + 1) * t_new, ls],
                                      o[(2 * p + 1) * t_new:(2 * p + 2) * t_new, ls]))
            o_ref[0] = jnp.concatenate(outs, axis=-1).astype(BF16)


def _dsa_attend(page_table, qbd, mask, bias, k_new, v_new, pool_k, pool_v, li):
    bd_, n_pages = page_table.shape
    rows = qbd.shape[1]
    t_new = rows // H_A
    ng = n_pages // PG
    kspec = lambda r: pl.BlockSpec(
        (1, 1, W_TOK, PAGE_SIZE),
        lambda b, ph, g, pt, r=r: (li, pt[b, jnp.where(ph == 0, g, ng - 1) * PG + r], 0, 0))
    vspec = lambda r: pl.BlockSpec(
        (1, 1, W_TOK, PAGE_SIZE),
        lambda b, ph, g, pt, r=r: (li, pt[b, jnp.where(ph == 0, 0, g) * PG + r], 0, 0))
    per_seq = lambda shape: pl.BlockSpec((1,) + shape, lambda b, ph, g, pt: (b,) + (0,) * len(shape))
    grid_spec = pltpu.PrefetchScalarGridSpec(
        num_scalar_prefetch=1,
        grid=(bd_, 2, ng),
        in_specs=[per_seq((rows, W_TOK)), per_seq((n_pages + 1, t_new, PAGE_SIZE)),
                  pl.BlockSpec((2, rows, PAGE_SIZE), lambda b, ph, g, pt: (0, 0, 0)),
                  per_seq((PAGE_SIZE, W_TOK)), per_seq((PAGE_SIZE, W_TOK))]
                 + [kspec(r) for r in range(PG)] + [vspec(r) for r in range(PG)],
        out_specs=per_seq((t_new, W_TOK)),
        scratch_shapes=[pltpu.VMEM((ng, rows, PG * PAGE_SIZE), F32), pltpu.VMEM((rows, LANE), F32),
                        pltpu.VMEM((rows, LANE), F32), pltpu.VMEM((rows, LANE), F32),
                        pltpu.VMEM((rows, W_TOK), F32), pltpu.VMEM((W_TOK, PG * PAGE_SIZE), BF16)],
    )
    return pl.pallas_call(
        functools.partial(_dsa_attend_kernel, n_pages),
        grid_spec=grid_spec,
        out_shape=jax.ShapeDtypeStruct((bd_, t_new, W_TOK), BF16),
        compiler_params=_cparams(("parallel", "arbitrary", "arbitrary")),
        name="dsa_sample_attend",
    )(page_table, qbd, mask, bias, k_new, v_new, *([pool_k] * PG), *([pool_v] * PG))


def _dsa_sample(outs, bias_s, pool_k, pool_v, pool_ki, li, page_table):
    kf, vf, kif, qb, kb, vb, qib, ki2, wis, qmb = outs
    bd_, n_pages = page_table.shape
    t_new = kf.shape[1] // bd_
    n_keep = min(TOPK_MAX, (n_pages * PAGE_SIZE + t_new) // 4)
    qis = qib.reshape(bd_, t_new, H_IDX, D_IDX).transpose(0, 2, 1, 3).reshape(bd_, H_IDX * t_new, D_IDX)
    pad_rows = lambda a: jnp.pad(a.reshape(bd_, t_new, a.shape[-1]), ((0, 0), (0, PAGE_SIZE - t_new), (0, 0)))
    mask = _dsa_select(page_table, qis, wis.reshape(bd_, t_new, H_IDX), pad_rows(ki2[..., :D_IDX]),
                       jnp.swapaxes(pool_ki, 2, 3), li, n_keep)
    q4 = qb.reshape(bd_, t_new, H_A, HEAD_DIM)
    qbd = jnp.einsum("bthd,hg->bhtgd", q4, jnp.eye(H_A, dtype=q4.dtype)).reshape(bd_, H_A * t_new, W_TOK)
    page_t = lambda a: jnp.transpose(a, (0, 1, 3, 4, 2)).reshape(a.shape[:2] + (W_TOK, PAGE_SIZE))
    return _dsa_attend(page_table, qbd, mask, bias_s, pad_rows(kb), pad_rows(vb), page_t(pool_k), page_t(pool_v), li)


def _trunk(x, p, pw, s5w, mem_k, mem_v, attend, conv_state, ssm_state, tm, flat):
    b, t, _ = x.shape
    shp = (1, b * t) if flat else (b, t)
    fl = lambda a: a.reshape(shp + a.shape[2:])
    unfl = lambda a: a.reshape((b, t) + a.shape[2:])
    bdm = pw["bd"][:W_MEM, :W_MEM]
    ks, vs, kis, convs, res, ims = [], [], [], [], [], []
    for l in range(DEPTH):
        kind, li = l % N_MIXERS, l // N_MIXERS
        if kind == 0:
            outs = _in_a(fl(x), pw["norm_mix"][l], pw["w_in_a"][li], pw["bd"], pw["qn_a"][li], pw["kn_a"][li],
                         pw["qn_mem"][l], tm)
            mix = unfl(attend(li, outs))
            ks.append(outs[0].reshape(b, t, H_A, HEAD_DIM))
            vs.append(outs[1].reshape(b, t, H_A, HEAD_DIM))
            kis.append(unfl(outs[2]))
            qm = unfl(outs[9])
        elif kind == 1:
            mix, qm, nst = _in_b(x, pw["norm_mix"][l], pw["w_in_b"][li], bdm, pw["qn_mem"][l], p["conv_w"][li],
                                 conv_state(li), min(tm, t))
            convs.append(nst)
        else:
            mix, qm, hout = _s5(x, pw["norm_mix"][l], pw["w_in_c"][li], bdm, pw["qn_mem"][l], *s5w[li],
                                ssm_state(li), min(tm // 2, t))
            res.append(hout[:, 0].reshape(b, SSM_G, SSM_P))
            ims.append(hout[:, 1].reshape(b, SSM_G, SSM_P))
        mo = _mem_attn(qm, mem_k[l], mem_v[l], min(tm, t))
        x = unfl(_post(fl(x), fl(mix), fl(mo), pw["w_out"][l], pw["norm_mlp"][l], pw["w_up"][l], pw["w_down"][l], tm))
    return x, jnp.stack(ks), jnp.stack(vs), jnp.stack(kis), jnp.stack(convs), jnp.stack(res), jnp.stack(ims)


def kernel(x_prompt, x_sample, cache_k, cache_v, cache_kidx, state_conv, state_ssm_re, state_ssm_im,
           cache_mem_k, cache_mem_v, page_table, mem_prompt, norm_mix, norm_mem, norm_mlp,
           w_in_a, w_in_b, w_in_c, w_out, qn_a, kn_a, rel_bias, conv_w, ssm_a_re, ssm_a_im, ssm_log_dt,
           ssm_b_re, ssm_b_im, ssm_c_re, ssm_c_im, ssm_d, w_glu, b_glu, w_mem_kv, qn_mem, kn_mem, w_up, w_down):
    p = dict(norm_mix=norm_mix, norm_mem=norm_mem, norm_mlp=norm_mlp, w_in_a=w_in_a, w_in_b=w_in_b,
             w_in_c=w_in_c, w_out=w_out, qn_a=qn_a, kn_a=kn_a, conv_w=conv_w, ssm_a_re=ssm_a_re,
             ssm_a_im=ssm_a_im, ssm_log_dt=ssm_log_dt, ssm_b_re=ssm_b_re, ssm_b_im=ssm_b_im, ssm_c_re=ssm_c_re,
             ssm_c_im=ssm_c_im, ssm_d=ssm_d, w_glu=w_glu, b_glu=b_glu, w_mem_kv=w_mem_kv, qn_mem=qn_mem,
             kn_mem=kn_mem, w_up=w_up, w_down=w_down)
    pw = _prep_weights(p)
    n_c = ssm_a_re.shape[0]
    s5w = [_s5_prep(p, li) for li in range(n_c)]
    bp, sp, _ = x_prompt.shape
    bd_, td, _ = x_sample.shape

    bias_p = _bias_tiles(rel_bias, jnp.asarray(_prompt_buckets()))
    pmk, pmv, pmk_b, pmv_b = _mem_kv(mem_prompt, pw["norm_mem"], pw["w_mem_kv"], pw["bd"][:W_MEM, :W_MEM],
                                     pw["kn_mem"])

    def attend_prompt(li, outs):
        kf, vf, kif, qb, kb, vb, qib, ki2, wis, qmb = outs
        return _dsa_prompt(qib, jnp.swapaxes(wis, 1, 2), ki2, qb, kb, vb, bias_p, min(TOPK_MAX, sp // 4))

    y_prompt, pk, pv, pki, pconv, pre, pim = _trunk(
        x_prompt, p, pw, s5w, pmk_b, pmv_b, attend_prompt,
        lambda li: jnp.zeros((bp, CONV_W - 1, W_TOK), F32),
        lambda li: jnp.zeros((bp, 2, SSM_N), F32), 512, False)

    bias_s = _bias_tiles(rel_bias, jnp.asarray(_sample_buckets(td)))
    bias_s = jnp.swapaxes(bias_s, 0, 1).reshape(2, H_A * td, PAGE_SIZE)
    mem_t = lambda a: jnp.transpose(a, (0, 1, 3, 4, 2)).reshape(DEPTH, bd_, W_MEM, N_MEM).astype(BF16)
    smk = mem_t(cache_mem_k)
    smv = mem_t(cache_mem_v)

    def attend_sample(li, outs):
        return _dsa_sample(outs, bias_s, cache_k, cache_v, cache_kidx, li, page_table)

    y_sample, sk, sv, ski, sconv, sre, sim = _trunk(
        x_sample, p, pw, s5w, smk, smv, attend_sample,
        lambda li: state_conv[li],
        lambda li: jnp.stack([state_ssm_re[li].reshape(bd_, SSM_N), state_ssm_im[li].reshape(bd_, SSM_N)], axis=1),
        bd_ * td, True)

    mem_out = lambda a: jnp.transpose(a.reshape(DEPTH, bp, H_MEM, HEAD_DIM, N_MEM), (0, 1, 4, 2, 3))
    return (y_prompt, y_sample, pk, pv, pki, pconv, pre, pim, mem_out(pmk), mem_out(pmv),
            sk, sv, ski, sconv, sre, sim)
```

```python
import functools
import math

import numpy as np
import jax
import jax.numpy as jnp
from jax import lax
from jax.experimental import pallas as pl
from jax.experimental.pallas import tpu as pltpu

D_MODEL = 1024
HEAD_DIM = 64
W_TOK = 768
W_MEM = 256
H_A = 12
H_MEM = 4
H_IDX = 8
D_IDX = 64
TOPK_MAX = 256
N_BUCKETS = 32
MAX_DISTANCE = 128
CONV_W = 3
SSM_GROUP = 16
SSM_G = 48
SSM_P = 64
SSM_N = SSM_G * SSM_P
D_FF = 4096
N_MEM = 256
PAGE_SIZE = 128
DEPTH = 4
N_MIXERS = 3
EPS = 1e-6
ATTN_SCALE = HEAD_DIM ** -0.5
IDX_SCALE = (H_IDX * D_IDX) ** -0.5

LANE = 128
QB = 256
VMEM_LIMIT = 56 * 1024 * 1024

BF16 = jnp.bfloat16
F32 = jnp.float32
NEG_INF = float("-inf")
INT_MIN = -2 ** 31
KEY_NEG_INF = int(np.array([0xFF800000], np.uint32).view(np.int32)[0]) ^ 0x7FFFFFFF


def _cparams(sem):
    return pltpu.CompilerParams(dimension_semantics=sem, vmem_limit_bytes=VMEM_LIMIT)


def _const_spec(shape):
    nd = len(shape)
    return pl.BlockSpec(shape, lambda *_: (0,) * nd, pipeline_mode=pl.Buffered(1))


def _dot(a, b):
    return jnp.dot(a, b, preferred_element_type=F32)


def _dot_t(a, b):
    return lax.dot_general(a, b, (((1,), (1,)), ((), ())), preferred_element_type=F32)


def _rms(x, g):
    ms = jnp.mean(x * x, axis=-1, keepdims=True)
    return x * lax.rsqrt(ms + EPS) * g


def _head_rms(x, g_tiled, bd):
    x2 = x * x
    hi = x2.astype(BF16)
    r1 = x2 - hi.astype(F32)
    mid = r1.astype(BF16)
    lo = (r1 - mid.astype(F32)).astype(BF16)
    ss = _dot(hi, bd) + _dot(mid, bd) + _dot(lo, bd)
    return x * lax.rsqrt(ss * (1.0 / HEAD_DIM) + EPS) * g_tiled


def _half_masks(rows):
    lane = lax.broadcasted_iota(jnp.int32, (rows, LANE), 1)
    return lane < HEAD_DIM, lane >= HEAD_DIM


A_Q, A_K, A_V, A_QI, A_QM, A_KI, A_WI, A_END = 0, 768, 1536, 2304, 2816, 3072, 3136, 3200


def _in_a_kernel(x_ref, g_ref, w_ref, bd_ref, qn_ref, kn_ref, mn_ref,
                 k_ref, v_ref, ki_ref, qb_ref, kb_ref, vb_ref, qib_ref, ki2_ref, wi_ref, qm_ref):
    h = _rms(x_ref[0], g_ref[...]).astype(BF16)
    proj = _dot(h, w_ref[...])
    bd = bd_ref[...]
    q = _head_rms(proj[:, A_Q:A_K], qn_ref[...], bd) * ATTN_SCALE
    k = _head_rms(proj[:, A_K:A_V], kn_ref[...], bd)
    v = proj[:, A_V:A_QI]
    ki = proj[:, A_KI:A_KI + D_IDX]
    qm = _head_rms(proj[:, A_QM:A_KI], mn_ref[...], bd[:W_MEM, :W_MEM]) * ATTN_SCALE
    k_ref[0] = k
    v_ref[0] = v
    ki_ref[0] = ki
    qb_ref[0] = q.astype(BF16)
    kb_ref[0] = k.astype(BF16)
    vb_ref[0] = v.astype(BF16)
    qib_ref[0] = proj[:, A_QI:A_QM].astype(BF16)
    ki2_ref[0] = jnp.concatenate([ki, ki], axis=-1).astype(BF16)
    wi_ref[0] = proj[:, A_WI:A_WI + H_IDX] * IDX_SCALE
    qm_ref[0] = qm.astype(BF16)


def _in_a(x, g, w, bd, qn, kn, mn, tm):
    b, t, _ = x.shape
    tok = lambda w_, dt: jax.ShapeDtypeStruct((b, t, w_), dt)
    tspec = lambda w_: pl.BlockSpec((1, tm, w_), lambda i, j: (i, j, 0))
    widths = [(W_TOK, F32), (W_TOK, F32), (D_IDX, F32), (W_TOK, BF16), (W_TOK, BF16), (W_TOK, BF16),
              (H_IDX * D_IDX, BF16), (2 * D_IDX, BF16), (H_IDX, F32), (W_MEM, BF16)]
    return pl.pallas_call(
        _in_a_kernel,
        grid=(b, t // tm),
        in_specs=[tspec(D_MODEL), _const_spec((1, D_MODEL)), _const_spec((D_MODEL, A_END)),
                  _const_spec((W_TOK, W_TOK)), _const_spec((1, W_TOK)), _const_spec((1, W_TOK)),
                  _const_spec((1, W_MEM))],
        out_specs=[tspec(w_) for w_, _ in widths],
        out_shape=[tok(w_, dt) for w_, dt in widths],
        compiler_params=_cparams(("parallel", "parallel")),
        name="in_proj_dsa",
    )(x, g, w, bd, qn, kn, mn)


def _in_b_kernel(x_ref, g_ref, w_ref, bd_ref, mn_ref, cw_ref, st_ref,
                 mix_ref, qm_ref, nst_ref, ubuf):
    j = pl.program_id(1)
    tm = x_ref.shape[1]

    @pl.when(j == 0)
    def _():
        ubuf[6:8, :] = st_ref[0]

    h = _rms(x_ref[0], g_ref[...]).astype(BF16)
    proj = _dot(h, w_ref[...])
    u = proj[:, 0:W_TOK]
    gb = proj[:, W_TOK:2 * W_TOK]
    gc = proj[:, 2 * W_TOK:3 * W_TOK]
    ubuf[8:8 + tm, :] = gc * u
    y = cw_ref[0:1, :] * ubuf[6:6 + tm, :]
    y = y + cw_ref[1:2, :] * ubuf[7:7 + tm, :]
    y = y + cw_ref[2:3, :] * ubuf[8:8 + tm, :]
    mix_ref[0] = (gb * y).astype(BF16)
    last = ubuf[6 + tm:8 + tm, :]
    nst_ref[0] = last
    ubuf[6:8, :] = last
    qm = _head_rms(proj[:, 3 * W_TOK:], mn_ref[...], bd_ref[...]) * ATTN_SCALE
    qm_ref[0] = qm.astype(BF16)


def _in_b(x, g, w, bd, mn, cw, state, tm):
    b, t, _ = x.shape
    tspec = lambda w_: pl.BlockSpec((1, tm, w_), lambda i, j: (i, j, 0))
    return pl.pallas_call(
        _in_b_kernel,
        grid=(b, t // tm),
        in_specs=[tspec(D_MODEL), _const_spec((1, D_MODEL)), _const_spec((D_MODEL, 3 * W_TOK + W_MEM)),
                  _const_spec((W_MEM, W_MEM)), _const_spec((1, W_MEM)), _const_spec((CONV_W, W_TOK)),
                  pl.BlockSpec((1, CONV_W - 1, W_TOK), lambda i, j: (i, 0, 0))],
        out_specs=[tspec(W_TOK), tspec(W_MEM), pl.BlockSpec((1, CONV_W - 1, W_TOK), lambda i, j: (i, 0, 0))],
        out_shape=[jax.ShapeDtypeStruct((b, t, W_TOK), BF16), jax.ShapeDtypeStruct((b, t, W_MEM), BF16),
                   jax.ShapeDtypeStruct((b, CONV_W - 1, W_TOK), F32)],
        scratch_shapes=[pltpu.VMEM((tm + 8, W_TOK), F32)],
        compiler_params=_cparams(("parallel", "arbitrary")),
        name="in_proj_conv",
    )(x, g, w, bd, mn, cw, state)


def _s5_disc_kernel(are_ref, aim_ref, ldt_ref, bre_ref, bim_ref, lam_ref, bbre_ref, bbim_ref):
    a_re = are_ref[...]
    a_im = aim_ref[...]
    dt = jnp.exp(ldt_ref[...])
    mag = jnp.exp(a_re * dt)
    l_re = mag * jnp.cos(a_im * dt)
    l_im = mag * jnp.sin(a_im * dt)
    x_re = l_re - 1.0
    den = a_re * a_re + a_im * a_im
    c_re = (x_re * a_re + l_im * a_im) / den
    c_im = (l_im * a_re - x_re * a_im) / den
    lam_ref[0] = l_re
    lam_ref[1] = l_im
    b_re = bre_ref[...]
    b_im = bim_ref[...]
    bbre_ref[...] = c_re[:, None, :] * b_re - c_im[:, None, :] * b_im
    bbim_ref[...] = c_re[:, None, :] * b_im + c_im[:, None, :] * b_re


def _s5_disc(a_re, a_im, log_dt, b_re_t, b_im_t):
    g3 = jax.ShapeDtypeStruct((SSM_G, SSM_GROUP, SSM_P), F32)
    return pl.pallas_call(
        _s5_disc_kernel,
        out_shape=[jax.ShapeDtypeStruct((2, SSM_G, SSM_P), F32), g3, g3],
        name="s5_discretise",
    )(a_re, a_im, log_dt.reshape(SSM_G, 1), b_re_t, b_im_t)


S5_LANES = 512
N_BT = SSM_N // QB
N_CT = W_TOK // QB


def _s5_kernel(x_ref, g_ref, w_ref, bd_ref, mn_ref, bwre_ref, bwim_ref, lam_ref, cwre_ref, cwim_ref,
               d_ref, wglu_ref, bglu_ref, h0_ref,
               mix_ref, qm_ref, hout_ref, sre, sim):
    j = pl.program_id(1)
    tt = x_ref.shape[1]

    @pl.when(j == 0)
    def _():
        hout_ref[0] = h0_ref[0]

    h = _rms(x_ref[0], g_ref[...]).astype(BF16)
    proj = _dot(h, w_ref[...])
    u = proj[:, :W_TOK]
    ub = u.astype(BF16)
    for jt in range(N_BT):
        uk = ub[:, LANE * (jt // 2):LANE * (jt // 2 + 1)]
        sre[:, QB * jt:QB * (jt + 1)] = _dot(uk, bwre_ref[jt])
        sim[:, QB * jt:QB * (jt + 1)] = _dot(uk, bwim_ref[jt])

    for lc in range(SSM_N // S5_LANES):
        sl = slice(lc * S5_LANES, (lc + 1) * S5_LANES)
        l_re = lam_ref[0:1, sl]
        l_im = lam_ref[1:2, sl]

        def step(i, carry, sl=sl, l_re=l_re, l_im=l_im):
            h_re, h_im = carry
            n_re = l_re * h_re - l_im * h_im + sre[pl.ds(i, 1), sl]
            n_im = l_re * h_im + l_im * h_re + sim[pl.ds(i, 1), sl]
            sre[pl.ds(i, 1), sl] = n_re
            sim[pl.ds(i, 1), sl] = n_im
            return n_re, n_im

        h_re, h_im = lax.fori_loop(0, tt, step, (hout_ref[0, 0:1, sl], hout_ref[0, 1:2, sl]))
        hout_ref[0, 0:1, sl] = h_re
        hout_ref[0, 1:2, sl] = h_im

    ys = []
    for jc in range(N_CT):
        ks = slice(jc * 4 * QB, (jc + 1) * 4 * QB)
        ys.append(_dot(sre[:, ks].astype(BF16), cwre_ref[jc]) - _dot(sim[:, ks].astype(BF16), cwim_ref[jc]))
    y = jnp.concatenate(ys, axis=-1) + d_ref[...] * u
    y = jax.nn.gelu(y)
    z = _dot(y.astype(BF16), wglu_ref[...]) + bglu_ref[...]
    mix_ref[0] = (y * (1.0 / (1.0 + jnp.exp(-z)))).astype(BF16)
    qm = _head_rms(proj[:, W_TOK:], mn_ref[...], bd_ref[...]) * ATTN_SCALE
    qm_ref[0] = qm.astype(BF16)


def _s5(x, g, w, bd, mn, bwre, bwim, lam, cwre, cwim, d, wglu, bglu, h0, tt):
    b, t, _ = x.shape
    tspec = lambda w_: pl.BlockSpec((1, tt, w_), lambda i, j: (i, j, 0))
    sspec = pl.BlockSpec((1, 2, SSM_N), lambda i, j: (i, 0, 0))
    return pl.pallas_call(
        _s5_kernel,
        grid=(b, t // tt),
        in_specs=[tspec(D_MODEL), _const_spec((1, D_MODEL)), _const_spec((D_MODEL, W_TOK + W_MEM)),
                  _const_spec((W_MEM, W_MEM)), _const_spec((1, W_MEM)),
                  _const_spec((N_BT, LANE, QB)), _const_spec((N_BT, LANE, QB)), _const_spec((2, SSM_N)),
                  _const_spec((N_CT, 4 * QB, QB)), _const_spec((N_CT, 4 * QB, QB)),
                  _const_spec((1, W_TOK)), _const_spec((W_TOK, W_TOK)), _const_spec((1, W_TOK)), sspec],
        out_specs=[tspec(W_TOK), tspec(W_MEM), sspec],
        out_shape=[jax.ShapeDtypeStruct((b, t, W_TOK), BF16), jax.ShapeDtypeStruct((b, t, W_MEM), BF16),
                   jax.ShapeDtypeStruct((b, 2, SSM_N), F32)],
        scratch_shapes=[pltpu.VMEM((tt, SSM_N), F32), pltpu.VMEM((tt, SSM_N), F32)],
        compiler_params=_cparams(("parallel", "arbitrary")),
        name="s5_mixer",
    )(x, g, w, bd, mn, bwre, bwim, lam, cwre, cwim, d, wglu, bglu, h0)


def _s5_weights(b_bar_re, b_bar_im, c_re, c_im):
    eye = jnp.eye(SSM_G, dtype=F32)

    def b_tiles(bb):
        full = jnp.einsum("gcp,gh->gchp", bb, eye).reshape(W_TOK, SSM_N)
        return jnp.stack([full[LANE * (j // 2):LANE * (j // 2 + 1), QB * j:QB * (j + 1)]
                          for j in range(N_BT)]).astype(BF16)

    def c_tiles(cc):
        full = jnp.einsum("gcp,gh->gphc", cc, eye).reshape(SSM_N, W_TOK)
        return jnp.stack([full[4 * QB * j:4 * QB * (j + 1), QB * j:QB * (j + 1)]
                          for j in range(N_CT)]).astype(BF16)

    return b_tiles(b_bar_re), b_tiles(b_bar_im), c_tiles(c_re), c_tiles(c_im)


def _mem_kv_kernel(m_ref, g_ref, w_ref, bd_ref, kn_ref, k_ref, v_ref, kb_ref, vb_ref):
    h = _rms(m_ref[0], g_ref[0]).astype(BF16)
    kv = _dot(h, w_ref[0])
    k = _head_rms(kv[:, :W_MEM], kn_ref[0], bd_ref[...]).T
    v = kv[:, W_MEM:].T
    k_ref[0, 0] = k
    v_ref[0, 0] = v
    kb_ref[0, 0] = k.astype(BF16)
    vb_ref[0, 0] = v.astype(BF16)


def _mem_kv(mem, g, w, bd, kn):
    b = mem.shape[0]
    ospec = pl.BlockSpec((1, 1, N_MEM, W_MEM), lambda l, i: (l, i, 0, 0))
    osh = lambda dt: jax.ShapeDtypeStruct((DEPTH, b, N_MEM, W_MEM), dt)
    return pl.pallas_call(
        _mem_kv_kernel,
        grid=(DEPTH, b),
        in_specs=[pl.BlockSpec((1, N_MEM, D_MODEL), lambda l, i: (i, 0, 0)),
                  pl.BlockSpec((1, 1, D_MODEL), lambda l, i: (l, 0, 0)),
                  pl.BlockSpec((1, D_MODEL, 2 * W_MEM), lambda l, i: (l, 0, 0)),
                  _const_spec((W_MEM, W_MEM)),
                  pl.BlockSpec((1, 1, W_MEM), lambda l, i: (l, 0, 0))],
        out_specs=[ospec] * 4,
        out_shape=[osh(F32), osh(F32), osh(BF16), osh(BF16)],
        compiler_params=_cparams(("parallel", "parallel")),
        name="mem_kv",
    )(mem, g, w, bd, kn)


def _mem_attn_kernel(qm_ref, k_ref, v_ref, o_ref):
    tm = qm_ref.shape[1]
    lo, hi = _half_masks(tm)
    outs = []
    for p in range(H_MEM // 2):
        ls = slice(p * LANE, (p + 1) * LANE)
        qp = qm_ref[0, :, ls]
        kp = k_ref[0, ls, :]
        vp = v_ref[0, ls, :]
        halves = []
        for msk in (lo, hi):
            s = _dot(jnp.where(msk, qp, jnp.zeros_like(qp)), kp)
            e = jnp.exp(s - jnp.max(s, axis=-1, keepdims=True))
            o = _dot_t(e.astype(BF16), vp)
            halves.append(o / jnp.sum(e, axis=-1, keepdims=True))
        outs.append(jnp.where(lo, halves[0], halves[1]))
    o_ref[0] = jnp.concatenate(outs, axis=-1).astype(BF16)


def _mem_attn(qm, k, v, tm):
    b, t, _ = qm.shape
    kvspec = pl.BlockSpec((1, N_MEM, W_MEM), lambda i, j: (i, 0, 0))
    tspec = pl.BlockSpec((1, tm, W_MEM), lambda i, j: (i, j, 0))
    return pl.pallas_call(
        _mem_attn_kernel,
        grid=(b, t // tm),
        in_specs=[tspec, kvspec, kvspec],
        out_specs=tspec,
        out_shape=jax.ShapeDtypeStruct((b, t, W_MEM), BF16),
        compiler_params=_cparams(("parallel", "parallel")),
        name="mem_attn",
    )(qm, k, v)


FF_CHUNK = 1024


def _post_kernel(x_ref, mix_ref, mo_ref, wo_ref, g_ref, wu_ref, wd_ref, o_ref):
    o_ref[0] = x_ref[0] + _dot(mix_ref[0], wo_ref[:W_TOK, :]) + _dot(mo_ref[0], wo_ref[W_TOK:, :])
    hm = _rms(o_ref[0], g_ref[...]).astype(BF16)
    for f in range(D_FF // FF_CHUNK):
        fs = slice(f * FF_CHUNK, (f + 1) * FF_CHUNK)
        a = jnp.maximum(_dot(hm, wu_ref[:, fs]), 0.0)
        o_ref[0] += _dot((a * a).astype(BF16), wd_ref[fs, :])


def _post(x, mix, mo, wo, g, wu, wd, tm):
    b, t, _ = x.shape
    tspec = lambda w_: pl.BlockSpec((1, tm, w_), lambda i, j: (i, j, 0))
    return pl.pallas_call(
        _post_kernel,
        grid=(b, t // tm),
        in_specs=[tspec(D_MODEL), tspec(W_TOK), tspec(W_MEM), _const_spec((D_MODEL, D_MODEL)),
                  _const_spec((1, D_MODEL)), _const_spec((D_MODEL, D_FF)), _const_spec((D_FF, D_MODEL))],
        out_specs=tspec(D_MODEL),
        out_shape=jax.ShapeDtypeStruct((b, t, D_MODEL), F32),
        compiler_params=_cparams(("parallel", "parallel")),
        name="out_proj_mlp",
    )(x, mix, mo, wo, g, wu, wd)


def _bucket_table(n):
    d = np.arange(n)
    exact = N_BUCKETS // 2
    nf = np.maximum(d, 1).astype(np.float32)
    far = exact + (np.log(nf / np.float32(exact)) / np.float32(math.log(MAX_DISTANCE / exact))
                   * np.float32(N_BUCKETS - exact)).astype(np.int32)
    return np.where(d < exact, d, np.minimum(far, N_BUCKETS - 1)).astype(np.int32)


def _bias_kernel(rb_ref, bk_ref, o_ref):
    bk = bk_ref[...]
    for h in range(H_A):
        far = rb_ref[N_BUCKETS - 1, h]
        for i in range(bk.shape[0]):
            t = jnp.zeros(bk.shape[1:], F32)
            for b in range(N_BUCKETS - 1):
                t = jnp.where(bk[i] == b, rb_ref[b, h] - far, t)
            o_ref[h, i] = t


def _bias_tiles(rel_bias, buckets):
    return pl.pallas_call(
        _bias_kernel,
        in_specs=[pl.BlockSpec(memory_space=pltpu.SMEM), pl.BlockSpec(memory_space=pltpu.VMEM)],
        out_shape=jax.ShapeDtypeStruct((H_A,) + buckets.shape, F32),
        compiler_params=pltpu.CompilerParams(vmem_limit_bytes=VMEM_LIMIT),
        name="rel_bias_tiles",
    )(rel_bias, buckets)


def _prompt_buckets():
    table = _bucket_table(2 * QB)
    tq = np.arange(QB)[:, None]
    sk = np.arange(QB)[None, :]
    diag = table[np.maximum(tq - sk, 0)]
    off = table[QB + tq - sk]
    return np.stack([off, diag]).astype(np.int32)


def _sort_key(s):
    s = jnp.where(s == 0.0, 0.0, s)
    bits = pltpu.bitcast(s, jnp.int32)
    return jnp.where(bits < 0, bits ^ jnp.int32(0x7FFFFFFF), bits)


def _key_to_float(key):
    bits = jnp.where(key < 0, key ^ jnp.int32(0x7FFFFFFF), key)
    return jnp.where(key < KEY_NEG_INF, NEG_INF, pltpu.bitcast(bits, F32))


def _dsa_prompt_kernel(n_keep, qi_ref, wit_ref, ki2_ref, q_ref, k_ref, v_ref, bias_ref, o_ref,
                       sbuf, mbuf, qh_ref, mx_ref, l_ref, acc_ref):
    j = pl.program_id(1)
    nc = j + 1
    lo, hi = _half_masks(QB)
    zero_b = jnp.zeros((QB, LANE), BF16)
    chunk = lambda c: pl.ds(pl.multiple_of(c * QB, QB), QB)

    qi = qi_ref[0]
    qim = [jnp.where(hi if h % 2 else lo, qi[:, LANE * (h // 2):LANE * (h // 2 + 1)], zero_b)
           for h in range(H_IDX)]
    krow = lax.broadcasted_iota(jnp.int32, (QB, QB), 0)
    qcol = lax.broadcasted_iota(jnp.int32, (QB, QB), 1)

    def scores(c):
        kc = ki2_ref[0, chunk(c), :]
        acc = jnp.zeros((QB, QB), F32)
        for h in range(H_IDX):
            acc = acc + wit_ref[0, h:h + 1, :] * jnp.maximum(_dot_t(kc, qim[h]), 0.0)
        return acc

    def score_chunk(c, _):
        sbuf[chunk(c), :] = scores(c)
        return 0

    lax.fori_loop(0, j, score_chunk, 0)
    sbuf[chunk(j), :] = jnp.where(krow <= qcol, scores(j), NEG_INF)

    def count(pred):
        def body(c, acc):
            hit = jnp.where(pred(sbuf[chunk(c), :], c * QB + krow), 1, 0)
            return acc + jnp.sum(hit.reshape(QB // 8, 8, QB), axis=0)
        acc = lax.fori_loop(0, nc, body, jnp.zeros((8, QB), jnp.int32))
        return jnp.sum(acc, axis=0, keepdims=True)

    def bisect(i, carry):
        t, cnt_t = carry
        cand = t + lax.shift_left(jnp.int32(1), 31 - i)
        cand_f = _key_to_float(cand)
        cnt = count(lambda blk, idx: blk >= cand_f)
        ok = cnt >= n_keep
        return jnp.where(ok, cand, t), jnp.where(ok, cnt, cnt_t)

    t, cnt_t = lax.fori_loop(0, 32, bisect, (jnp.full((1, QB), INT_MIN, jnp.int32),
                                             jnp.full((1, QB), 2 ** 30, jnp.int32)))
    t = _key_to_float(t)

    tied = jnp.logical_and(cnt_t > n_keep, t > NEG_INF)
    any_tied = jnp.max(jnp.where(tied, 1, 0)) > 0

    def tie_cut():
        need = n_keep - count(lambda blk, idx: blk > t)

        def step(i, x):
            cand = x + lax.shift_left(jnp.int32(1), 12 - i)
            cnt = count(lambda blk, idx: jnp.logical_and(blk == t, idx < cand))
            return jnp.where(cnt < need, cand, x)

        return lax.fori_loop(0, 13, step, jnp.zeros((1, QB), jnp.int32))

    jcut = lax.cond(any_tied, tie_cut, lambda: jnp.full((1, QB), 2 ** 30, jnp.int32))

    def mask_chunk(c, _):
        s = sbuf[chunk(c), :]
        idx = c * QB + krow
        m = jnp.where(s > t, 0.0, jnp.where(s == t, jnp.where(idx <= jcut, 0.0, NEG_INF), NEG_INF))
        m = jnp.where(s > NEG_INF, m, NEG_INF)
        mbuf[c] = m.T
        return 0

    lax.fori_loop(0, nc, mask_chunk, 0)

    for p in range(H_A // 2):
        qp = q_ref[0, :, p * LANE:(p + 1) * LANE]
        qh_ref[p, :QB, :] = jnp.where(lo, qp, zero_b)
        qh_ref[p, QB:, :] = jnp.where(hi, qp, zero_b)
    mx_ref[...] = jnp.full(mx_ref.shape, NEG_INF, F32)

    def logits(c, p, near):
        r = _dot_t(qh_ref[p], k_ref[0, chunk(c), p * LANE:(p + 1) * LANE])
        out = []
        for hh in range(2):
            s = r[hh * QB:(hh + 1) * QB] + mbuf[c]
            if near is not None:
                s = s + bias_ref[2 * p + hh, near]
            out.append(s)
        return out

    def pass1(c, near):
        for p in range(H_A // 2):
            for hh, s in enumerate(logits(c, p, near)):
                h = 2 * p + hh
                mx_ref[h] = jnp.maximum(mx_ref[h], jnp.maximum(s[:, :LANE], s[:, LANE:]))

    def pass2(c, near):
        for p in range(H_A // 2):
            es = []
            for hh, s in enumerate(logits(c, p, near)):
                h = 2 * p + hh
                m = mx_ref[h]
                e0 = jnp.exp(s[:, :LANE] - m)
                e1 = jnp.exp(s[:, LANE:] - m)
                l_ref[h] += e0 + e1
                es.append(jnp.concatenate([e0, e1], axis=-1).astype(BF16))
            r = _dot(jnp.concatenate(es, axis=0), v_ref[0, chunk(c), p * LANE:(p + 1) * LANE])
            acc_ref[2 * p] += r[:QB]
            acc_ref[2 * p + 1] += r[QB:]

    def sweep(fn):
        def far(c, _):
            fn(c, None)
            return 0

        lax.fori_loop(0, jnp.maximum(j - 1, 0), far, 0)

        @pl.when(j >= 1)
        def _():
            fn(j - 1, 0)

        fn(j, 1)

    sweep(pass1)
    for h in range(H_A):
        mx_ref[h] = jnp.broadcast_to(jnp.max(mx_ref[h], axis=-1, keepdims=True), (QB, LANE))
    l_ref[...] = jnp.zeros(l_ref.shape, F32)
    acc_ref[...] = jnp.zeros(acc_ref.shape, F32)
    sweep(pass2)
    for p in range(H_A // 2):
        o = [acc_ref[2 * p + hh] / jnp.sum(l_ref[2 * p + hh], axis=-1, keepdims=True) for hh in range(2)]
        o_ref[0, :, p * LANE:(p + 1) * LANE] = jnp.where(lo, o[0], o[1]).astype(BF16)


def _dsa_prompt(qi, wit, ki2, q, k, v, bias, n_keep):
    b, t, _ = q.shape
    nq = t // QB
    qspec = lambda w_: pl.BlockSpec((1, QB, w_), lambda i, j: (i, j, 0))
    kspec = lambda w_: pl.BlockSpec((1, t, w_), lambda i, j: (i, 0, 0))
    stat = pltpu.VMEM((H_A, QB, LANE), F32)
    return pl.pallas_call(
        functools.partial(_dsa_prompt_kernel, n_keep),
        grid=(b, nq),
        in_specs=[qspec(H_IDX * D_IDX), pl.BlockSpec((1, H_IDX, QB), lambda i, j: (i, 0, j)),
                  kspec(2 * D_IDX), qspec(W_TOK), kspec(W_TOK), kspec(W_TOK),
                  _const_spec((H_A, 2, QB, QB))],
        out_specs=qspec(W_TOK),
        out_shape=jax.ShapeDtypeStruct((b, t, W_TOK), BF16),
        scratch_shapes=[pltpu.VMEM((t, QB), F32), pltpu.VMEM((nq, QB, QB), F32),
                        pltpu.VMEM((H_A // 2, 2 * QB, LANE), BF16), stat, stat, stat],
        compiler_params=_cparams(("parallel", "arbitrary")),
        name="dsa_prompt",
    )(qi, wit, ki2, q, k, v, bias)


def _prep_weights(p):
    seg = np.arange(W_TOK) // HEAD_DIM
    bd = jnp.asarray(seg[:, None] == seg[None, :], BF16)
    row = lambda a: a.reshape(a.shape[0], 1, a.shape[-1])
    tile = lambda a, n: row(jnp.tile(a, (1, n)))
    wa = p["w_in_a"]
    pad = jnp.zeros(wa.shape[:2] + (A_END - A_WI - H_IDX,), wa.dtype)
    wa = jnp.concatenate([wa[..., 0:2304], wa[..., 2304:2816], wa[..., 2888:3144], wa[..., 2816:2880],
                          wa[..., 2880:2888], pad], axis=-1)
    return {
        "bd": bd,
        "norm_mix": row(p["norm_mix"]), "norm_mem": row(p["norm_mem"]), "norm_mlp": row(p["norm_mlp"]),
        "w_in_a": wa.astype(BF16), "w_in_b": p["w_in_b"].astype(BF16), "w_in_c": p["w_in_c"].astype(BF16),
        "w_out": p["w_out"].astype(BF16), "w_up": p["w_up"].astype(BF16), "w_down": p["w_down"].astype(BF16),
        "w_mem_kv": p["w_mem_kv"].astype(BF16),
        "qn_a": tile(p["qn_a"], H_A), "kn_a": tile(p["kn_a"], H_A),
        "qn_mem": tile(p["qn_mem"], H_MEM), "kn_mem": tile(p["kn_mem"], H_MEM),
    }


def _s5_prep(p, li):
    lam, bb_re, bb_im = _s5_disc(p["ssm_a_re"][li], p["ssm_a_im"][li], p["ssm_log_dt"][li],
                                 jnp.swapaxes(p["ssm_b_re"][li], 1, 2), jnp.swapaxes(p["ssm_b_im"][li], 1, 2))
    bwre, bwim, cwre, cwim = _s5_weights(bb_re, bb_im, p["ssm_c_re"][li], p["ssm_c_im"][li])
    return (bwre, bwim, lam.reshape(2, SSM_N), cwre, cwim, p["ssm_d"][li].reshape(1, W_TOK),
            p["w_glu"][li].astype(BF16), p["b_glu"][li].reshape(1, W_TOK))


PG = 8


def _sample_buckets(t_new):
    table = _bucket_table(2 * PAGE_SIZE + t_new)
    t = np.arange(t_new)[:, None]
    lane = np.arange(PAGE_SIZE)[None, :]
    return np.stack([table[PAGE_SIZE + t - lane], table[np.maximum(t - lane, 0)]]).astype(np.int32)


def _dsa_select_kernel(n_keep, n_pages, pt_ref, qis_ref, wi_ref, kin_ref, *rest):
    pages = rest[:PG]
    mask_ref, sbuf = rest[PG], rest[PG + 1]
    g = pl.program_id(1)
    t_new = wi_ref.shape[1]
    qis = qis_ref[0]
    wi = wi_ref[0]

    def scores(d):
        acc = jnp.zeros((t_new, PAGE_SIZE), F32)
        for h in range(H_IDX):
            acc = acc + wi[:, h:h + 1] * jnp.maximum(d[h * t_new:(h + 1) * t_new, :], 0.0)
        return acc

    for r in range(PG):
        sbuf[g * PG + r] = _sort_key(scores(_dot(qis, pages[r][0, 0].astype(BF16))))

    @pl.when(g == pl.num_programs(1) - 1)
    def _():
        trow = lax.broadcasted_iota(jnp.int32, (t_new, PAGE_SIZE), 0)
        lane = lax.broadcasted_iota(jnp.int32, (t_new, PAGE_SIZE), 1)
        sbuf[n_pages] = _sort_key(jnp.where(lane <= trow, scores(_dot_t(qis, kin_ref[0])), NEG_INF))

        def count(pred):
            acc = jnp.zeros((t_new, PAGE_SIZE), jnp.int32)
            for blk in range(n_pages + 1):
                acc = acc + jnp.where(pred(sbuf[blk], blk * PAGE_SIZE + lane), 1, 0)
            return jnp.sum(acc, axis=-1, keepdims=True)

        def bisect(i, carry):
            t, cnt_t = carry
            cand = t + lax.shift_left(jnp.int32(1), 31 - i)
            cnt = count(lambda blk, idx: blk >= cand)
            ok = cnt >= n_keep
            return jnp.where(ok, cand, t), jnp.where(ok, cnt, cnt_t)

        t, cnt_t = lax.fori_loop(0, 32, bisect, (jnp.full((t_new, 1), INT_MIN, jnp.int32),
                                                 jnp.full((t_new, 1), 2 ** 30, jnp.int32)))
        tied = jnp.logical_and(cnt_t > n_keep, t > KEY_NEG_INF)
        any_tied = jnp.max(jnp.where(tied, 1, 0)) > 0

        def tie_cut():
            need = n_keep - count(lambda blk, idx: blk > t)

            def step(i, x):
                cand = x + lax.shift_left(jnp.int32(1), 14 - i)
                cnt = count(lambda blk, idx: jnp.logical_and(blk == t, idx < cand))
                return jnp.where(cnt < need, cand, x)

            return lax.fori_loop(0, 15, step, jnp.zeros((t_new, 1), jnp.int32))

        jcut = lax.cond(any_tied, tie_cut, lambda: jnp.full((t_new, 1), 2 ** 30, jnp.int32))
        for blk in range(n_pages + 1):
            key = sbuf[blk]
            idx = blk * PAGE_SIZE + lane
            m = jnp.where(key > t, 0.0, jnp.where(key == t, jnp.where(idx <= jcut, 0.0, NEG_INF), NEG_INF))
            mask_ref[0, blk] = jnp.where(key > KEY_NEG_INF, m, NEG_INF)


def _dsa_select(page_table, qis, wi, ki_new, pool_ki, li, n_keep):
    bd_, n_pages = page_table.shape
    t_new = wi.shape[1]
    page_spec = lambda r: pl.BlockSpec((1, 1, D_IDX, PAGE_SIZE),
                                       lambda b, g, pt, r=r: (li, pt[b, g * PG + r], 0, 0))
    grid_spec = pltpu.PrefetchScalarGridSpec(
        num_scalar_prefetch=1,
        grid=(bd_, n_pages // PG),
        in_specs=[pl.BlockSpec((1, H_IDX * t_new, D_IDX), lambda b, g, pt: (b, 0, 0)),
                  pl.BlockSpec((1, t_new, H_IDX), lambda b, g, pt: (b, 0, 0)),
                  pl.BlockSpec((1, PAGE_SIZE, D_IDX), lambda b, g, pt: (b, 0, 0))]
                 + [page_spec(r) for r in range(PG)],
        out_specs=pl.BlockSpec((1, n_pages + 1, t_new, PAGE_SIZE), lambda b, g, pt: (b, 0, 0, 0)),
        scratch_shapes=[pltpu.VMEM((n_pages + 1, t_new, PAGE_SIZE), jnp.int32)],
    )
    return pl.pallas_call(
        functools.partial(_dsa_select_kernel, n_keep, n_pages),
        grid_spec=grid_spec,
        out_shape=jax.ShapeDtypeStruct((bd_, n_pages + 1, t_new, PAGE_SIZE), F32),
        compiler_params=_cparams(("parallel", "arbitrary")),
        name="dsa_sample_select",
    )(page_table, qis, wi, ki_new, *([pool_ki] * PG))


def _dsa_attend_kernel(n_pages, pt_ref, qbd_ref, mask_ref, bias_ref, kn_ref, vn_ref, *rest):
    kpages, vpages = rest[:PG], rest[PG:2 * PG]
    o_ref, lbuf, lnew, mx_ref, l_ref, acc_ref, stage = rest[2 * PG:]
    ph = pl.program_id(1)
    g = pl.program_id(2)
    ng = pl.num_programs(2)
    t_new = mask_ref.shape[2]
    rows = H_A * t_new
    qbd = qbd_ref[0]

    def fold(x):
        return [x[:, i * LANE:(i + 1) * LANE] for i in range(x.shape[1] // LANE)]

    def head_tile(m):
        return jnp.tile(m, (H_A, 1))

    @pl.when(ph == 0)
    def _():
        @pl.when(g == 0)
        def _():
            mx_ref[...] = jnp.full((rows, LANE), NEG_INF, F32)

        for r in range(PG):
            stage[:, r * PAGE_SIZE:(r + 1) * PAGE_SIZE] = kpages[r][0, 0].astype(BF16)
        s = _dot(qbd, stage[...])
        parts = []
        for r in range(PG):
            part = s[:, r * LANE:(r + 1) * LANE] + head_tile(mask_ref[0, g * PG + r])
            if r == PG - 1:
                part = part + jnp.where(g == ng - 1, 1.0, 0.0) * bias_ref[0]
            parts.append(part)
        lbuf[g] = jnp.concatenate(parts, axis=-1)
        mx = mx_ref[...]
        for part in parts:
            mx = jnp.maximum(mx, part)
        mx_ref[...] = mx

        @pl.when(g == ng - 1)
        def _():
            sn = _dot_t(qbd, kn_ref[0]) + head_tile(mask_ref[0, n_pages]) + bias_ref[1]
            lnew[...] = sn
            mx_ref[...] = jnp.maximum(mx_ref[...], sn)

    @pl.when(ph == 1)
    def _():
        m = jnp.max(mx_ref[...], axis=-1, keepdims=True)

        @pl.when(g == 0)
        def _():
            en = jnp.exp(lnew[...] - m)
            l_ref[...] = en
            acc_ref[...] = _dot(en.astype(BF16), vn_ref[0])

        for r in range(PG):
            stage[:, r * PAGE_SIZE:(r + 1) * PAGE_SIZE] = vpages[r][0, 0].astype(BF16)
        e = jnp.exp(lbuf[g] - m)
        lsum = l_ref[...]
        for part in fold(e):
            lsum = lsum + part
        l_ref[...] = lsum
        acc_ref[...] += _dot_t(e.astype(BF16), stage[...])

        @pl.when(g == ng - 1)
        def _():
            o = acc_ref[...] / jnp.sum(l_ref[...], axis=-1, keepdims=True)
            lo, _ = _half_masks(t_new)
            outs = []
            for p in range(H_A // 2):
                ls = slice(p * LANE, (p + 1) * LANE)
                outs.append(jnp.where(lo, o[2 * p * t_new:(2 * p + 1) * t_new, ls],
                                      o[(2 * p + 1) * t_new:(2 * p + 2) * t_new, ls]))
            o_ref[0] = jnp.concatenate(outs, axis=-1).astype(BF16)


def _dsa_attend(page_table, qbd, mask, bias, k_new, v_new, pool_k, pool_v, li):
    bd_, n_pages = page_table.shape
    rows = qbd.shape[1]
    t_new = rows // H_A
    ng = n_pages // PG
    kspec = lambda r: pl.BlockSpec(
        (1, 1, W_TOK, PAGE_SIZE),
        lambda b, ph, g, pt, r=r: (li, pt[b, jnp.where(ph == 0, g, ng - 1) * PG + r], 0, 0))
    vspec = lambda r: pl.BlockSpec(
        (1, 1, W_TOK, PAGE_SIZE),
        lambda b, ph, g, pt, r=r: (li, pt[b, jnp.where(ph == 0, 0, g) * PG + r], 0, 0))
    per_seq = lambda shape: pl.BlockSpec((1,) + shape, lambda b, ph, g, pt: (b,) + (0,) * len(shape))
    grid_spec = pltpu.PrefetchScalarGridSpec(
        num_scalar_prefetch=1,
        grid=(bd_, 2, ng),
        in_specs=[per_seq((rows, W_TOK)), per_seq((n_pages + 1, t_new, PAGE_SIZE)),
                  pl.BlockSpec((2, rows, PAGE_SIZE), lambda b, ph, g, pt: (0, 0, 0)),
                  per_seq((PAGE_SIZE, W_TOK)), per_seq((PAGE_SIZE, W_TOK))]
                 + [kspec(r) for r in range(PG)] + [vspec(r) for r in range(PG)],
        out_specs=per_seq((t_new, W_TOK)),
        scratch_shapes=[pltpu.VMEM((ng, rows, PG * PAGE_SIZE), F32), pltpu.VMEM((rows, LANE), F32),
                        pltpu.VMEM((rows, LANE), F32), pltpu.VMEM((rows, LANE), F32),
                        pltpu.VMEM((rows, W_TOK), F32), pltpu.VMEM((W_TOK, PG * PAGE_SIZE), BF16)],
    )
    return pl.pallas_call(
        functools.partial(_dsa_attend_kernel, n_pages),
        grid_spec=grid_spec,
        out_shape=jax.ShapeDtypeStruct((bd_, t_new, W_TOK), BF16),
        compiler_params=_cparams(("parallel", "arbitrary", "arbitrary")),
        name="dsa_sample_attend",
    )(page_table, qbd, mask, bias, k_new, v_new, *([pool_k] * PG), *([pool_v] * PG))


def _dsa_sample(outs, bias_s, pool_k, pool_v, pool_ki, li, page_table):
    kf, vf, kif, qb, kb, vb, qib, ki2, wis, qmb = outs
    bd_, n_pages = page_table.shape
    t_new = kf.shape[1] // bd_
    n_keep = min(TOPK_MAX, (n_pages * PAGE_SIZE + t_new) // 4)
    qis = qib.reshape(bd_, t_new, H_IDX, D_IDX).transpose(0, 2, 1, 3).reshape(bd_, H_IDX * t_new, D_IDX)
    pad_rows = lambda a: jnp.pad(a.reshape(bd_, t_new, a.shape[-1]), ((0, 0), (0, PAGE_SIZE - t_new), (0, 0)))
    mask = _dsa_select(page_table, qis, wis.reshape(bd_, t_new, H_IDX), pad_rows(ki2[..., :D_IDX]),
                       jnp.swapaxes(pool_ki, 2, 3), li, n_keep)
    q4 = qb.reshape(bd_, t_new, H_A, HEAD_DIM)
    qbd = jnp.einsum("bthd,hg->bhtgd", q4, jnp.eye(H_A, dtype=q4.dtype)).reshape(bd_, H_A * t_new, W_TOK)
    page_t = lambda a: jnp.transpose(a, (0, 1, 3, 4, 2)).reshape(a.shape[:2] + (W_TOK, PAGE_SIZE))
    return _dsa_attend(page_table, qbd, mask, bias_s, pad_rows(kb), pad_rows(vb), page_t(pool_k), page_t(pool_v), li)


def _trunk(x, p, pw, s5w, mem_k, mem_v, attend, conv_state, ssm_state, tm, flat):
    b, t, _ = x.shape
    shp = (1, b * t) if flat else (b, t)
    fl = lambda a: a.reshape(shp + a.shape[2:])
    unfl = lambda a: a.reshape((b, t) + a.shape[2:])
    bdm = pw["bd"][:W_MEM, :W_MEM]
    ks, vs, kis, convs, res, ims = [], [], [], [], [], []
    for l in range(DEPTH):
        kind, li = l % N_MIXERS, l // N_MIXERS
        if kind == 0:
            outs = _in_a(fl(x), pw["norm_mix"][l], pw["w_in_a"][li], pw["bd"], pw["qn_a"][li], pw["kn_a"][li],
                         pw["qn_mem"][l], tm)
            mix = unfl(attend(li, outs))
            ks.append(outs[0].reshape(b, t, H_A, HEAD_DIM))
            vs.append(outs[1].reshape(b, t, H_A, HEAD_DIM))
            kis.append(unfl(outs[2]))
            qm = unfl(outs[9])
        elif kind == 1:
            mix, qm, nst = _in_b(x, pw["norm_mix"][l], pw["w_in_b"][li], bdm, pw["qn_mem"][l], p["conv_w"][li],
                                 conv_state(li), min(tm, t))
            convs.append(nst)
        else:
            mix, qm, hout = _s5(x, pw["norm_mix"][l], pw["w_in_c"][li], bdm, pw["qn_mem"][l], *s5w[li],
                                ssm_state(li), min(tm // 2, t))
            res.append(hout[:, 0].reshape(b, SSM_G, SSM_P))
            ims.append(hout[:, 1].reshape(b, SSM_G, SSM_P))
        mo = _mem_attn(qm, mem_k[l], mem_v[l], min(tm, t))
        x = unfl(_post(fl(x), fl(mix), fl(mo), pw["w_out"][l], pw["norm_mlp"][l], pw["w_up"][l], pw["w_down"][l], tm))
    return x, jnp.stack(ks), jnp.stack(vs), jnp.stack(kis), jnp.stack(convs), jnp.stack(res), jnp.stack(ims)


def kernel(x_prompt, x_sample, cache_k, cache_v, cache_kidx, state_conv, state_ssm_re, state_ssm_im,
           cache_mem_k, cache_mem_v, page_table, mem_prompt, norm_mix, norm_mem, norm_mlp,
           w_in_a, w_in_b, w_in_c, w_out, qn_a, kn_a, rel_bias, conv_w, ssm_a_re, ssm_a_im, ssm_log_dt,
           ssm_b_re, ssm_b_im, ssm_c_re, ssm_c_im, ssm_d, w_glu, b_glu, w_mem_kv, qn_mem, kn_mem, w_up, w_down):
    p = dict(norm_mix=norm_mix, norm_mem=norm_mem, norm_mlp=norm_mlp, w_in_a=w_in_a, w_in_b=w_in_b,
             w_in_c=w_in_c, w_out=w_out, qn_a=qn_a, kn_a=kn_a, conv_w=conv_w, ssm_a_re=ssm_a_re,
             ssm_a_im=ssm_a_im, ssm_log_dt=ssm_log_dt, ssm_b_re=ssm_b_re, ssm_b_im=ssm_b_im, ssm_c_re=ssm_c_re,
             ssm_c_im=ssm_c_im, ssm_d=ssm_d, w_glu=w_glu, b_glu=b_glu, w_mem_kv=w_mem_kv, qn_mem=qn_mem,
             kn_mem=kn_mem, w_up=w_up, w_down=w_down)
    pw = _prep_weights(p)
    n_c = ssm_a_re.shape[0]
    s5w = [_s5_prep(p, li) for li in range(n_c)]
    bp, sp, _ = x_prompt.shape
    bd_, td, _ = x_sample.shape

    bias_p = _bias_tiles(rel_bias, jnp.asarray(_prompt_buckets()))
    pmk, pmv, pmk_b, pmv_b = _mem_kv(mem_prompt, pw["norm_mem"], pw["w_mem_kv"], pw["bd"][:W_MEM, :W_MEM],
                                     pw["kn_mem"])

    def attend_prompt(li, outs):
        kf, vf, kif, qb, kb, vb, qib, ki2, wis, qmb = outs
        return _dsa_prompt(qib, jnp.swapaxes(wis, 1, 2), ki2, qb, kb, vb, bias_p, min(TOPK_MAX, sp // 4))

    y_prompt, pk, pv, pki, pconv, pre, pim = _trunk(
        x_prompt, p, pw, s5w, pmk_b, pmv_b, attend_prompt,
        lambda li: jnp.zeros((bp, CONV_W - 1, W_TOK), F32),
        lambda li: jnp.zeros((bp, 2, SSM_N), F32), 512, False)

    bias_s = _bias_tiles(rel_bias, jnp.asarray(_sample_buckets(td)))
    bias_s = jnp.swapaxes(bias_s, 0, 1).reshape(2, H_A * td, PAGE_SIZE)
    mem_t = lambda a: jnp.transpose(a, (0, 1, 3, 4, 2)).reshape(DEPTH, bd_, W_MEM, N_MEM).astype(BF16)
    smk = mem_t(cache_mem_k)
    smv = mem_t(cache_mem_v)

    def attend_sample(li, outs):
        return _dsa_sample(outs, bias_s, cache_k, cache_v, cache_kidx, li, page_table)

    y_sample, sk, sv, ski, sconv, sre, sim = _trunk(
        x_sample, p, pw, s5w, smk, smv, attend_sample,
        lambda li: state_conv[li],
        lambda li: jnp.stack([state_ssm_re[li].reshape(bd_, SSM_N), state_ssm_im[li].reshape(bd_, SSM_N)], axis=1),
        bd_ * td, True)

    mem_out = lambda a: jnp.transpose(a.reshape(DEPTH, bp, H_MEM, HEAD_DIM, N_MEM), (0, 1, 4, 2, 3))
    return (y_prompt, y_sample, pk, pv, pki, pconv, pre, pim, mem_out(pmk), mem_out(pmv),
            sk, sv, ski, sconv, sre, sim)
```

```python
import functools
import math

import numpy as np
import jax
import jax.numpy as jnp
from jax import lax
from jax.experimental import pallas as pl
from jax.experimental.pallas import tpu as pltpu

D_MODEL = 1024
HEAD_DIM = 64
W_TOK = 768
W_MEM = 256
H_A = 12
H_MEM = 4
H_IDX = 8
D_IDX = 64
TOPK_MAX = 256
N_BUCKETS = 32
MAX_DISTANCE = 128
CONV_W = 3
SSM_GROUP = 16
SSM_G = 48
SSM_P = 64
SSM_N = SSM_G * SSM_P
D_FF = 4096
N_MEM = 256
PAGE_SIZE = 128
DEPTH = 4
N_MIXERS = 3
EPS = 1e-6
ATTN_SCALE = HEAD_DIM ** -0.5
IDX_SCALE = (H_IDX * D_IDX) ** -0.5

LANE = 128
QB = 256
VMEM_LIMIT = 56 * 1024 * 1024

BF16 = jnp.bfloat16
F32 = jnp.float32
NEG_INF = float("-inf")
INT_MIN = -2 ** 31
KEY_NEG_INF = int(np.array([0xFF800000], np.uint32).view(np.int32)[0]) ^ 0x7FFFFFFF


def _cparams(sem):
    return pltpu.CompilerParams(dimension_semantics=sem, vmem_limit_bytes=VMEM_LIMIT)


def _const_spec(shape):
    nd = len(shape)
    return pl.BlockSpec(shape, lambda *_: (0,) * nd, pipeline_mode=pl.Buffered(1))


def _dot(a, b):
    return jnp.dot(a, b, preferred_element_type=F32)


def _dot_t(a, b):
    return lax.dot_general(a, b, (((1,), (1,)), ((), ())), preferred_element_type=F32)


def _rms(x, g):
    ms = jnp.mean(x * x, axis=-1, keepdims=True)
    return x * lax.rsqrt(ms + EPS) * g


def _head_rms(x, g_tiled, bd):
    x2 = x * x
    hi = x2.astype(BF16)
    lo = (x2 - hi.astype(F32)).astype(BF16)
    b2 = bd[:LANE, :LANE]
    ss = jnp.concatenate([_dot(hi[:, i:i + LANE], b2) + _dot(lo[:, i:i + LANE], b2)
                          for i in range(0, x.shape[1], LANE)], axis=-1)
    return x * lax.rsqrt(ss * (1.0 / HEAD_DIM) + EPS) * g_tiled


def _half_masks(rows):
    lane = lax.broadcasted_iota(jnp.int32, (rows, LANE), 1)
    return lane < HEAD_DIM, lane >= HEAD_DIM


A_Q, A_K, A_V, A_QI, A_QM, A_KI, A_WI, A_END = 0, 768, 1536, 2304, 2816, 3072, 3136, 3200


def _in_a_kernel(x_ref, g_ref, w_ref, bd_ref, qn_ref, kn_ref, mn_ref,
                 k_ref, v_ref, ki_ref, qb_ref, kb_ref, vb_ref, qib_ref, ki2_ref, wi_ref, qm_ref):
    h = _rms(x_ref[0], g_ref[...]).astype(BF16)
    proj = _dot(h, w_ref[...])
    bd = bd_ref[...]
    q = _head_rms(proj[:, A_Q:A_K], qn_ref[...], bd) * ATTN_SCALE
    k = _head_rms(proj[:, A_K:A_V], kn_ref[...], bd)
    v = proj[:, A_V:A_QI]
    ki = proj[:, A_KI:A_KI + D_IDX]
    qm = _head_rms(proj[:, A_QM:A_KI], mn_ref[...], bd[:W_MEM, :W_MEM]) * ATTN_SCALE
    k_ref[0] = k
    v_ref[0] = v
    ki_ref[0] = ki
    qb_ref[0] = q.astype(BF16)
    kb_ref[0] = k.astype(BF16)
    vb_ref[0] = v.astype(BF16)
    qib_ref[0] = proj[:, A_QI:A_QM].astype(BF16)
    ki2_ref[0] = jnp.concatenate([ki, ki], axis=-1).astype(BF16)
    wi_ref[0] = proj[:, A_WI:A_WI + H_IDX] * IDX_SCALE
    qm_ref[0] = qm.astype(BF16)


def _in_a(x, g, w, bd, qn, kn, mn, tm):
    b, t, _ = x.shape
    tok = lambda w_, dt: jax.ShapeDtypeStruct((b, t, w_), dt)
    tspec = lambda w_: pl.BlockSpec((1, tm, w_), lambda i, j: (i, j, 0))
    widths = [(W_TOK, F32), (W_TOK, F32), (D_IDX, F32), (W_TOK, BF16), (W_TOK, BF16), (W_TOK, BF16),
              (H_IDX * D_IDX, BF16), (2 * D_IDX, BF16), (H_IDX, F32), (W_MEM, BF16)]
    return pl.pallas_call(
        _in_a_kernel,
        grid=(b, t // tm),
        in_specs=[tspec(D_MODEL), _const_spec((1, D_MODEL)), _const_spec((D_MODEL, A_END)),
                  _const_spec((W_TOK, W_TOK)), _const_spec((1, W_TOK)), _const_spec((1, W_TOK)),
                  _const_spec((1, W_MEM))],
        out_specs=[tspec(w_) for w_, _ in widths],
        out_shape=[tok(w_, dt) for w_, dt in widths],
        compiler_params=_cparams(("parallel", "parallel")),
        name="in_proj_dsa",
    )(x, g, w, bd, qn, kn, mn)


def _in_b_kernel(x_ref, g_ref, w_ref, bd_ref, mn_ref, cw_ref, st_ref,
                 mix_ref, qm_ref, nst_ref, ubuf):
    j = pl.program_id(1)
    tm = x_ref.shape[1]

    @pl.when(j == 0)
    def _():
        ubuf[6:8, :] = st_ref[0]

    h = _rms(x_ref[0], g_ref[...]).astype(BF16)
    proj = _dot(h, w_ref[...])
    u = proj[:, 0:W_TOK]
    gb = proj[:, W_TOK:2 * W_TOK]
    gc = proj[:, 2 * W_TOK:3 * W_TOK]
    ubuf[8:8 + tm, :] = gc * u
    y = cw_ref[0:1, :] * ubuf[6:6 + tm, :]
    y = y + cw_ref[1:2, :] * ubuf[7:7 + tm, :]
    y = y + cw_ref[2:3, :] * ubuf[8:8 + tm, :]
    mix_ref[0] = (gb * y).astype(BF16)
    last = ubuf[6 + tm:8 + tm, :]
    nst_ref[0] = last
    ubuf[6:8, :] = last
    qm = _head_rms(proj[:, 3 * W_TOK:], mn_ref[...], bd_ref[...]) * ATTN_SCALE
    qm_ref[0] = qm.astype(BF16)


def _in_b(x, g, w, bd, mn, cw, state, tm):
    b, t, _ = x.shape
    tspec = lambda w_: pl.BlockSpec((1, tm, w_), lambda i, j: (i, j, 0))
    return pl.pallas_call(
        _in_b_kernel,
        grid=(b, t // tm),
        in_specs=[tspec(D_MODEL), _const_spec((1, D_MODEL)), _const_spec((D_MODEL, 3 * W_TOK + W_MEM)),
                  _const_spec((W_MEM, W_MEM)), _const_spec((1, W_MEM)), _const_spec((CONV_W, W_TOK)),
                  pl.BlockSpec((1, CONV_W - 1, W_TOK), lambda i, j: (i, 0, 0))],
        out_specs=[tspec(W_TOK), tspec(W_MEM), pl.BlockSpec((1, CONV_W - 1, W_TOK), lambda i, j: (i, 0, 0))],
        out_shape=[jax.ShapeDtypeStruct((b, t, W_TOK), BF16), jax.ShapeDtypeStruct((b, t, W_MEM), BF16),
                   jax.ShapeDtypeStruct((b, CONV_W - 1, W_TOK), F32)],
        scratch_shapes=[pltpu.VMEM((tm + 8, W_TOK), F32)],
        compiler_params=_cparams(("parallel", "arbitrary")),
        name="in_proj_conv",
    )(x, g, w, bd, mn, cw, state)


def _s5_disc_kernel(are_ref, aim_ref, ldt_ref, bre_ref, bim_ref, lam_ref, bbre_ref, bbim_ref):
    a_re = are_ref[...]
    a_im = aim_ref[...]
    dt = jnp.exp(ldt_ref[...])
    mag = jnp.exp(a_re * dt)
    l_re = mag * jnp.cos(a_im * dt)
    l_im = mag * jnp.sin(a_im * dt)
    x_re = l_re - 1.0
    den = a_re * a_re + a_im * a_im
    c_re = (x_re * a_re + l_im * a_im) / den
    c_im = (l_im * a_re - x_re * a_im) / den
    p_re, p_im = l_re, l_im
    for i in range(S5_STEPS):
        lam_ref[0, i] = p_re
        lam_ref[1, i] = p_im
        p_re, p_im = p_re * l_re - p_im * l_im, p_re * l_im + p_im * l_re
    b_re = bre_ref[...]
    b_im = bim_ref[...]
    bbre_ref[...] = c_re[:, None, :] * b_re - c_im[:, None, :] * b_im
    bbim_ref[...] = c_re[:, None, :] * b_im + c_im[:, None, :] * b_re


def _s5_disc(a_re, a_im, log_dt, b_re_t, b_im_t):
    g3 = jax.ShapeDtypeStruct((SSM_G, SSM_GROUP, SSM_P), F32)
    return pl.pallas_call(
        _s5_disc_kernel,
        out_shape=[jax.ShapeDtypeStruct((2, S5_STEPS, SSM_G, SSM_P), F32), g3, g3],
        name="s5_discretise",
    )(a_re, a_im, log_dt.reshape(SSM_G, 1), b_re_t, b_im_t)


S5_LANES = 512
S5_SEG = 8
S5_STEPS = 32
N_BT = SSM_N // QB
N_CT = W_TOK // QB


def _scan_rows(sre, sim, lam_ref, hout_ref, tt):
    for lc in range(SSM_N // S5_LANES):
        sl = slice(lc * S5_LANES, (lc + 1) * S5_LANES)
        l_re = lam_ref[0, 0:1, sl]
        l_im = lam_ref[1, 0:1, sl]

        def step(i, carry, sl=sl, l_re=l_re, l_im=l_im):
            h_re, h_im = carry
            n_re = l_re * h_re - l_im * h_im + sre[pl.ds(i, 1), sl]
            n_im = l_re * h_im + l_im * h_re + sim[pl.ds(i, 1), sl]
            sre[pl.ds(i, 1), sl] = n_re
            sim[pl.ds(i, 1), sl] = n_im
            return n_re, n_im

        h_re, h_im = lax.fori_loop(0, tt, step, (hout_ref[0, 0:1, sl], hout_ref[0, 1:2, sl]))
        hout_ref[0, 0:1, sl] = h_re
        hout_ref[0, 1:2, sl] = h_im


def _scan_segments(sre, sim, lam_ref, hout_ref):
    for lc in range(SSM_N // S5_LANES):
        sl = slice(lc * S5_LANES, (lc + 1) * S5_LANES)
        l_re = jnp.broadcast_to(lam_ref[0, 0:1, sl], (S5_SEG, S5_LANES))
        l_im = jnp.broadcast_to(lam_ref[1, 0:1, sl], (S5_SEG, S5_LANES))
        rows = lambda i: pl.ds(pl.multiple_of(i * S5_SEG, S5_SEG), S5_SEG)

        def step(i, carry, sl=sl, l_re=l_re, l_im=l_im):
            h_re, h_im = carry
            n_re = l_re * h_re - l_im * h_im + sre[rows(i), sl]
            n_im = l_re * h_im + l_im * h_re + sim[rows(i), sl]
            sre[rows(i), sl] = n_re
            sim[rows(i), sl] = n_im
            return n_re, n_im

        zero = jnp.zeros((S5_SEG, S5_LANES), F32)
        e_re, e_im = lax.fori_loop(0, S5_STEPS, step, (zero, zero))
        p_re = lam_ref[0, S5_STEPS - 1:S5_STEPS, sl]
        p_im = lam_ref[1, S5_STEPS - 1:S5_STEPS, sl]
        c_re = hout_ref[0, 0:1, sl]
        c_im = hout_ref[0, 1:2, sl]
        in_re, in_im = [], []
        for s in range(S5_SEG):
            in_re.append(c_re)
            in_im.append(c_im)
            c_re, c_im = (e_re[s:s + 1] + p_re * c_re - p_im * c_im, e_im[s:s + 1] + p_re * c_im + p_im * c_re)
        hout_ref[0, 0:1, sl] = c_re
        hout_ref[0, 1:2, sl] = c_im
        hin_re = jnp.concatenate(in_re, axis=0)
        hin_im = jnp.concatenate(in_im, axis=0)

        def fix(i, _, sl=sl, hin_re=hin_re, hin_im=hin_im):
            q_re = lam_ref[0, pl.ds(i, 1), sl]
            q_im = lam_ref[1, pl.ds(i, 1), sl]
            sre[rows(i), sl] += q_re * hin_re - q_im * hin_im
            sim[rows(i), sl] += q_re * hin_im + q_im * hin_re
            return 0

        lax.fori_loop(0, S5_STEPS, fix, 0)


def _s5_kernel(segmented, x_ref, g_ref, w_ref, bd_ref, mn_ref, bwre_ref, bwim_ref, lam_ref, perm_ref,
               cwre_ref, cwim_ref, d_ref, wglu_ref, bglu_ref, h0_ref,
               mix_ref, qm_ref, hout_ref, sre, sim):
    j = pl.program_id(1)
    tt = x_ref.shape[1]

    @pl.when(j == 0)
    def _():
        hout_ref[0] = h0_ref[0]

    x = x_ref[0]
    if segmented:
        hi = x.astype(BF16)
        r1 = x - hi.astype(F32)
        mid = r1.astype(BF16)
        lo = (r1 - mid.astype(F32)).astype(BF16)
        x = _dot(perm_ref[0], hi) + _dot(perm_ref[0], mid) + _dot(perm_ref[0], lo)
    h = _rms(x, g_ref[...]).astype(BF16)
    proj = _dot(h, w_ref[...])
    u = proj[:, :W_TOK]
    ub = u.astype(BF16)
    for jt in range(N_BT):
        uk = ub[:, LANE * (jt // 2):LANE * (jt // 2 + 1)]
        sre[:, QB * jt:QB * (jt + 1)] = _dot(uk, bwre_ref[jt])
        sim[:, QB * jt:QB * (jt + 1)] = _dot(uk, bwim_ref[jt])

    if segmented:
        _scan_segments(sre, sim, lam_ref, hout_ref)
    else:
        _scan_rows(sre, sim, lam_ref, hout_ref, tt)

    ys = []
    for jc in range(N_CT):
        ks = slice(jc * 4 * QB, (jc + 1) * 4 * QB)
        ys.append(_dot(sre[:, ks].astype(BF16), cwre_ref[jc]) - _dot(sim[:, ks].astype(BF16), cwim_ref[jc]))
    y = jnp.concatenate(ys, axis=-1) + d_ref[...] * u
    y = jax.nn.gelu(y)
    z = _dot(y.astype(BF16), wglu_ref[...]) + bglu_ref[...]
    mix = (y * (1.0 / (1.0 + jnp.exp(-z)))).astype(BF16)
    qm = (_head_rms(proj[:, W_TOK:], mn_ref[...], bd_ref[...]) * ATTN_SCALE).astype(BF16)
    if segmented:
        mix = _dot(perm_ref[1], mix).astype(BF16)
        qm = _dot(perm_ref[1], qm).astype(BF16)
    mix_ref[0] = mix
    qm_ref[0] = qm


def _s5(x, g, w, bd, mn, bwre, bwim, lam, cwre, cwim, d, wglu, bglu, h0, tt):
    b, t, _ = x.shape
    segmented = tt == S5_SEG * S5_STEPS
    r = np.arange(tt)
    back = (r % S5_STEPS)[:, None] * S5_SEG + (r // S5_STEPS)[:, None] == r[None, :]
    perm = jnp.asarray(np.stack([back.T, back]), BF16)
    tspec = lambda w_: pl.BlockSpec((1, tt, w_), lambda i, j: (i, j, 0))
    sspec = pl.BlockSpec((1, 2, SSM_N), lambda i, j: (i, 0, 0))
    return pl.pallas_call(
        functools.partial(_s5_kernel, segmented),
        grid=(b, t // tt),
        in_specs=[tspec(D_MODEL), _const_spec((1, D_MODEL)), _const_spec((D_MODEL, W_TOK + W_MEM)),
                  _const_spec((W_MEM, W_MEM)), _const_spec((1, W_MEM)),
                  _const_spec((N_BT, LANE, QB)), _const_spec((N_BT, LANE, QB)),
                  _const_spec((2, S5_STEPS, SSM_N)), _const_spec((2, tt, tt)),
                  _const_spec((N_CT, 4 * QB, QB)), _const_spec((N_CT, 4 * QB, QB)),
                  _const_spec((1, W_TOK)), _const_spec((W_TOK, W_TOK)), _const_spec((1, W_TOK)), sspec],
        out_specs=[tspec(W_TOK), tspec(W_MEM), sspec],
        out_shape=[jax.ShapeDtypeStruct((b, t, W_TOK), BF16), jax.ShapeDtypeStruct((b, t, W_MEM), BF16),
                   jax.ShapeDtypeStruct((b, 2, SSM_N), F32)],
        scratch_shapes=[pltpu.VMEM((tt, SSM_N), F32), pltpu.VMEM((tt, SSM_N), F32)],
        compiler_params=_cparams(("parallel", "arbitrary")),
        name="s5_mixer",
    )(x, g, w, bd, mn, bwre, bwim, lam, perm, cwre, cwim, d, wglu, bglu, h0)


def _s5_weights(b_bar_re, b_bar_im, c_re, c_im):
    eye = jnp.eye(SSM_G, dtype=F32)

    def b_tiles(bb):
        full = jnp.einsum("gcp,gh->gchp", bb, eye).reshape(W_TOK, SSM_N)
        return jnp.stack([full[LANE * (j // 2):LANE * (j // 2 + 1), QB * j:QB * (j + 1)]
                          for j in range(N_BT)]).astype(BF16)

    def c_tiles(cc):
        full = jnp.einsum("gcp,gh->gphc", cc, eye).reshape(SSM_N, W_TOK)
        return jnp.stack([full[4 * QB * j:4 * QB * (j + 1), QB * j:QB * (j + 1)]
                          for j in range(N_CT)]).astype(BF16)

    return b_tiles(b_bar_re), b_tiles(b_bar_im), c_tiles(c_re), c_tiles(c_im)


def _mem_kv_kernel(m_ref, g_ref, w_ref, bd_ref, kn_ref, k_ref, v_ref, kb_ref, vb_ref):
    h = _rms(m_ref[0], g_ref[0]).astype(BF16)
    kv = _dot(h, w_ref[0])
    k = _head_rms(kv[:, :W_MEM], kn_ref[0], bd_ref[...]).T
    v = kv[:, W_MEM:].T
    k_ref[0, 0] = k
    v_ref[0, 0] = v
    kb_ref[0, 0] = k.astype(BF16)
    vb_ref[0, 0] = v.astype(BF16)


def _mem_kv(mem, g, w, bd, kn):
    b = mem.shape[0]
    ospec = pl.BlockSpec((1, 1, N_MEM, W_MEM), lambda l, i: (l, i, 0, 0))
    osh = lambda dt: jax.ShapeDtypeStruct((DEPTH, b, N_MEM, W_MEM), dt)
    return pl.pallas_call(
        _mem_kv_kernel,
        grid=(DEPTH, b),
        in_specs=[pl.BlockSpec((1, N_MEM, D_MODEL), lambda l, i: (i, 0, 0)),
                  pl.BlockSpec((1, 1, D_MODEL), lambda l, i: (l, 0, 0)),
                  pl.BlockSpec((1, D_MODEL, 2 * W_MEM), lambda l, i: (l, 0, 0)),
                  _const_spec((W_MEM, W_MEM)),
                  pl.BlockSpec((1, 1, W_MEM), lambda l, i: (l, 0, 0))],
        out_specs=[ospec] * 4,
        out_shape=[osh(F32), osh(F32), osh(BF16), osh(BF16)],
        compiler_params=_cparams(("parallel", "parallel")),
        name="mem_kv",
    )(mem, g, w, bd, kn)


def _mem_attn_kernel(qm_ref, k_ref, v_ref, o_ref):
    tm = qm_ref.shape[1]
    lo, hi = _half_masks(tm)
    outs = []
    for p in range(H_MEM // 2):
        ls = slice(p * LANE, (p + 1) * LANE)
        qp = qm_ref[0, :, ls]
        kp = k_ref[0, ls, :]
        vp = v_ref[0, ls, :]
        halves = []
        for msk in (lo, hi):
            s = _dot(jnp.where(msk, qp, jnp.zeros_like(qp)), kp)
            e = jnp.exp(s - jnp.max(s, axis=-1, keepdims=True))
            o = _dot_t(e.astype(BF16), vp)
            halves.append(o / jnp.sum(e, axis=-1, keepdims=True))
        outs.append(jnp.where(lo, halves[0], halves[1]))
    o_ref[0] = jnp.concatenate(outs, axis=-1).astype(BF16)


def _mem_attn(qm, k, v, tm):
    b, t, _ = qm.shape
    kvspec = pl.BlockSpec((1, N_MEM, W_MEM), lambda i, j: (i, 0, 0))
    tspec = pl.BlockSpec((1, tm, W_MEM), lambda i, j: (i, j, 0))
    return pl.pallas_call(
        _mem_attn_kernel,
        grid=(b, t // tm),
        in_specs=[tspec, kvspec, kvspec],
        out_specs=tspec,
        out_shape=jax.ShapeDtypeStruct((b, t, W_MEM), BF16),
        compiler_params=_cparams(("parallel", "parallel")),
        name="mem_attn",
    )(qm, k, v)


FF_CHUNK = 1024


def _post_kernel(x_ref, mix_ref, mo_ref, wo_ref, g_ref, wu_ref, wd_ref, o_ref):
    o_ref[0] = x_ref[0] + _dot(mix_ref[0], wo_ref[:W_TOK, :]) + _dot(mo_ref[0], wo_ref[W_TOK:, :])
    hm = _rms(o_ref[0], g_ref[...]).astype(BF16)
    for f in range(D_FF // FF_CHUNK):
        fs = slice(f * FF_CHUNK, (f + 1) * FF_CHUNK)
        a = jnp.maximum(_dot(hm, wu_ref[:, fs]), 0.0)
        o_ref[0] += _dot((a * a).astype(BF16), wd_ref[fs, :])


def _post(x, mix, mo, wo, g, wu, wd, tm):
    b, t, _ = x.shape
    tspec = lambda w_: pl.BlockSpec((1, tm, w_), lambda i, j: (i, j, 0))
    return pl.pallas_call(
        _post_kernel,
        grid=(b, t // tm),
        in_specs=[tspec(D_MODEL), tspec(W_TOK), tspec(W_MEM), _const_spec((D_MODEL, D_MODEL)),
                  _const_spec((1, D_MODEL)), _const_spec((D_MODEL, D_FF)), _const_spec((D_FF, D_MODEL))],
        out_specs=tspec(D_MODEL),
        out_shape=jax.ShapeDtypeStruct((b, t, D_MODEL), F32),
        compiler_params=_cparams(("parallel", "parallel")),
        name="out_proj_mlp",
    )(x, mix, mo, wo, g, wu, wd)


def _bucket_table(n):
    d = np.arange(n)
    exact = N_BUCKETS // 2
    nf = np.maximum(d, 1).astype(np.float32)
    far = exact + (np.log(nf / np.float32(exact)) / np.float32(math.log(MAX_DISTANCE / exact))
                   * np.float32(N_BUCKETS - exact)).astype(np.int32)
    return np.where(d < exact, d, np.minimum(far, N_BUCKETS - 1)).astype(np.int32)


def _bias_kernel(rb_ref, bk_ref, o_ref):
    bk = bk_ref[...]
    for h in range(H_A):
        far = rb_ref[N_BUCKETS - 1, h]
        for i in range(bk.shape[0]):
            t = jnp.zeros(bk.shape[1:], F32)
            for b in range(N_BUCKETS - 1):
                t = jnp.where(bk[i] == b, rb_ref[b, h] - far, t)
            o_ref[h, i] = t


def _bias_tiles(rel_bias, buckets):
    return pl.pallas_call(
        _bias_kernel,
        in_specs=[pl.BlockSpec(memory_space=pltpu.SMEM), pl.BlockSpec(memory_space=pltpu.VMEM)],
        out_shape=jax.ShapeDtypeStruct((H_A,) + buckets.shape, F32),
        compiler_params=pltpu.CompilerParams(vmem_limit_bytes=VMEM_LIMIT),
        name="rel_bias_tiles",
    )(rel_bias, buckets)


def _prompt_buckets():
    table = _bucket_table(2 * QB)
    tq = np.arange(QB)[:, None]
    sk = np.arange(QB)[None, :]
    diag = table[np.maximum(tq - sk, 0)]
    off = table[QB + tq - sk]
    return np.stack([off, diag]).astype(np.int32)


def _sort_key(s):
    s = jnp.where(s == 0.0, 0.0, s)
    bits = pltpu.bitcast(s, jnp.int32)
    return jnp.where(bits < 0, bits ^ jnp.int32(0x7FFFFFFF), bits)


def _key_to_float(key):
    bits = jnp.where(key < 0, key ^ jnp.int32(0x7FFFFFFF), key)
    return jnp.where(key < KEY_NEG_INF, NEG_INF, pltpu.bitcast(bits, F32))


def _dsa_prompt_kernel(n_keep, qi_ref, wit_ref, ki2_ref, q_ref, k_ref, v_ref, bias_ref, o_ref,
                       sbuf, mbuf, qh_ref, mx_ref, l_ref, acc_ref):
    j = pl.program_id(1)
    nc = j + 1
    lo, hi = _half_masks(QB)
    zero_b = jnp.zeros((QB, LANE), BF16)
    chunk = lambda c: pl.ds(pl.multiple_of(c * QB, QB), QB)

    qi = qi_ref[0]
    qim = [jnp.where(hi if h % 2 else lo, qi[:, LANE * (h // 2):LANE * (h // 2 + 1)], zero_b)
           for h in range(H_IDX)]
    krow = lax.broadcasted_iota(jnp.int32, (QB, QB), 0)
    qcol = lax.broadcasted_iota(jnp.int32, (QB, QB), 1)

    def scores(c):
        kc = ki2_ref[0, chunk(c), :]
        acc = jnp.zeros((QB, QB), F32)
        for h in range(H_IDX):
            acc = acc + wit_ref[0, h:h + 1, :] * jnp.maximum(_dot_t(kc, qim[h]), 0.0)
        return acc

    def score_chunk(c, _):
        sbuf[chunk(c), :] = scores(c)
        return 0

    lax.fori_loop(0, j, score_chunk, 0)
    sbuf[chunk(j), :] = jnp.where(krow <= qcol, scores(j), NEG_INF)

    def count(pred):
        def body(c, acc):
            hit = jnp.where(pred(sbuf[chunk(c), :], c * QB + krow), 1, 0)
            return acc + jnp.sum(hit.reshape(QB // 8, 8, QB), axis=0)
        acc = lax.fori_loop(0, nc, body, jnp.zeros((8, QB), jnp.int32))
        return jnp.sum(acc, axis=0, keepdims=True)

    def bisect(carry):
        i, t, cnt_t, _ = carry
        cand = t + lax.shift_left(jnp.int32(1), 31 - i)
        cand_f = _key_to_float(cand)
        cnt = count(lambda blk, idx: blk >= cand_f)
        ok = cnt >= n_keep
        t = jnp.where(ok, cand, t)
        cnt_t = jnp.where(ok, cnt, cnt_t)
        return i + 1, t, cnt_t, jnp.max(jnp.where(cnt_t != n_keep, 1, 0))

    _, t, cnt_t, _ = lax.while_loop(
        lambda carry: jnp.logical_and(carry[0] < 32, carry[3] > 0), bisect,
        (jnp.int32(0), jnp.full((1, QB), INT_MIN, jnp.int32), jnp.full((1, QB), 2 ** 30, jnp.int32), jnp.int32(1)))
    t = _key_to_float(t)

    tied = jnp.logical_and(cnt_t > n_keep, t > NEG_INF)
    any_tied = jnp.max(jnp.where(tied, 1, 0)) > 0

    def tie_cut():
        need = n_keep - count(lambda blk, idx: blk > t)

        def step(i, x):
            cand = x + lax.shift_left(jnp.int32(1), 12 - i)
            cnt = count(lambda blk, idx: jnp.logical_and(blk == t, idx < cand))
            return jnp.where(cnt < need, cand, x)

        return lax.fori_loop(0, 13, step, jnp.zeros((1, QB), jnp.int32))

    jcut = lax.cond(any_tied, tie_cut, lambda: jnp.full((1, QB), 2 ** 30, jnp.int32))

    def mask_chunk(c, _):
        s = sbuf[chunk(c), :]
        idx = c * QB + krow
        m = jnp.where(s > t, 0.0, jnp.where(s == t, jnp.where(idx <= jcut, 0.0, NEG_INF), NEG_INF))
        m = jnp.where(s > NEG_INF, m, NEG_INF)
        mbuf[c] = m.T
        return 0

    lax.fori_loop(0, nc, mask_chunk, 0)

    for p in range(H_A // 2):
        qp = q_ref[0, :, p * LANE:(p + 1) * LANE]
        qh_ref[p, :QB, :] = jnp.where(lo, qp, zero_b)
        qh_ref[p, QB:, :] = jnp.where(hi, qp, zero_b)
    mx_ref[...] = jnp.full(mx_ref.shape, NEG_INF, F32)

    def logits(c, p, near):
        r = _dot_t(qh_ref[p], k_ref[0, chunk(c), p * LANE:(p + 1) * LANE])
        out = []
        for hh in range(2):
            s = r[hh * QB:(hh + 1) * QB] + mbuf[c]
            if near is not None:
                s = s + bias_ref[2 * p + hh, near]
            out.append(s)
        return out

    def attend(c, near):
        for p in range(H_A // 2):
            es, alphas = [], []
            for hh, s in enumerate(logits(c, p, near)):
                h = 2 * p + hh
                m_old = mx_ref[h]
                row_max = jnp.max(jnp.maximum(s[:, :LANE], s[:, LANE:]), axis=-1, keepdims=True)
                m_new = jnp.maximum(m_old, row_max)
                mx_ref[h] = m_new
                m_new = jnp.where(m_new == NEG_INF, 0.0, m_new)
                alpha = jnp.exp(m_old - m_new)
                e0 = jnp.exp(s[:, :LANE] - m_new)
                e1 = jnp.exp(s[:, LANE:] - m_new)
                l_ref[h] = alpha * l_ref[h] + (e0 + e1)
                es.append(jnp.concatenate([e0, e1], axis=-1).astype(BF16))
                alphas.append(alpha)
            r = _dot(jnp.concatenate(es, axis=0), v_ref[0, chunk(c), p * LANE:(p + 1) * LANE])
            acc_ref[2 * p] = alphas[0] * acc_ref[2 * p] + r[:QB]
            acc_ref[2 * p + 1] = alphas[1] * acc_ref[2 * p + 1] + r[QB:]

    l_ref[...] = jnp.zeros(l_ref.shape, F32)
    acc_ref[...] = jnp.zeros(acc_ref.shape, F32)

    def far(c, _):
        attend(c, None)
        return 0

    lax.fori_loop(0, jnp.maximum(j - 1, 0), far, 0)

    @pl.when(j >= 1)
    def _():
        attend(j - 1, 0)

    attend(j, 1)
    for p in range(H_A // 2):
        o = [acc_ref[2 * p + hh] / jnp.sum(l_ref[2 * p + hh], axis=-1, keepdims=True) for hh in range(2)]
        o_ref[0, :, p * LANE:(p + 1) * LANE] = jnp.where(lo, o[0], o[1]).astype(BF16)


def _dsa_prompt(qi, wit, ki2, q, k, v, bias, n_keep):
    b, t, _ = q.shape
    nq = t // QB
    qspec = lambda w_: pl.BlockSpec((1, QB, w_), lambda i, j: (i, j, 0))
    kspec = lambda w_: pl.BlockSpec((1, t, w_), lambda i, j: (i, 0, 0))
    stat = pltpu.VMEM((H_A, QB, LANE), F32)
    return pl.pallas_call(
        functools.partial(_dsa_prompt_kernel, n_keep),
        grid=(b, nq),
        in_specs=[qspec(H_IDX * D_IDX), pl.BlockSpec((1, H_IDX, QB), lambda i, j: (i, 0, j)),
                  kspec(2 * D_IDX), qspec(W_TOK), kspec(W_TOK), kspec(W_TOK),
                  _const_spec((H_A, 2, QB, QB))],
        out_specs=qspec(W_TOK),
        out_shape=jax.ShapeDtypeStruct((b, t, W_TOK), BF16),
        scratch_shapes=[pltpu.VMEM((t, QB), F32), pltpu.VMEM((nq, QB, QB), F32),
                        pltpu.VMEM((H_A // 2, 2 * QB, LANE), BF16), stat, stat, stat],
        compiler_params=_cparams(("parallel", "arbitrary")),
        name="dsa_prompt",
    )(qi, wit, ki2, q, k, v, bias)


def _prep_weights(p):
    seg = np.arange(W_TOK) // HEAD_DIM
    bd = jnp.asarray(seg[:, None] == seg[None, :], BF16)
    row = lambda a: a.reshape(a.shape[0], 1, a.shape[-1])
    tile = lambda a, n: row(jnp.tile(a, (1, n)))
    wa = p["w_in_a"]
    pad = jnp.zeros(wa.shape[:2] + (A_END - A_WI - H_IDX,), wa.dtype)
    wa = jnp.concatenate([wa[..., 0:2304], wa[..., 2304:2816], wa[..., 2888:3144], wa[..., 2816:2880],
                          wa[..., 2880:2888], pad], axis=-1)
    return {
        "bd": bd,
        "norm_mix": row(p["norm_mix"]), "norm_mem": row(p["norm_mem"]), "norm_mlp": row(p["norm_mlp"]),
        "w_in_a": wa.astype(BF16), "w_in_b": p["w_in_b"].astype(BF16), "w_in_c": p["w_in_c"].astype(BF16),
        "w_out": p["w_out"].astype(BF16), "w_up": p["w_up"].astype(BF16), "w_down": p["w_down"].astype(BF16),
        "w_mem_kv": p["w_mem_kv"].astype(BF16),
        "qn_a": tile(p["qn_a"], H_A), "kn_a": tile(p["kn_a"], H_A),
        "qn_mem": tile(p["qn_mem"], H_MEM), "kn_mem": tile(p["kn_mem"], H_MEM),
    }


def _s5_prep(p, li):
    lam, bb_re, bb_im = _s5_disc(p["ssm_a_re"][li], p["ssm_a_im"][li], p["ssm_log_dt"][li],
                                 jnp.swapaxes(p["ssm_b_re"][li], 1, 2), jnp.swapaxes(p["ssm_b_im"][li], 1, 2))
    bwre, bwim, cwre, cwim = _s5_weights(bb_re, bb_im, p["ssm_c_re"][li], p["ssm_c_im"][li])
    return (bwre, bwim, lam.reshape(2, S5_STEPS, SSM_N), cwre, cwim, p["ssm_d"][li].reshape(1, W_TOK),
            p["w_glu"][li].astype(BF16), p["b_glu"][li].reshape(1, W_TOK))


PG = 16


def _sample_buckets(t_new):
    table = _bucket_table(2 * PAGE_SIZE + t_new)
    t = np.arange(t_new)[:, None]
    lane = np.arange(PAGE_SIZE)[None, :]
    return np.stack([table[PAGE_SIZE + t - lane], table[np.maximum(t - lane, 0)]]).astype(np.int32)


def _dsa_select_kernel(n_keep, n_pages, pt_ref, qis_ref, wi_ref, kin_ref, *rest):
    pages = rest[:PG]
    mask_ref, sbuf = rest[PG], rest[PG + 1]
    g = pl.program_id(1)
    t_new = wi_ref.shape[1]
    qis = qis_ref[0]
    wi = wi_ref[0]

    def scores(d):
        acc = jnp.zeros((t_new, PAGE_SIZE), F32)
        for h in range(H_IDX):
            acc = acc + wi[:, h:h + 1] * jnp.maximum(d[h * t_new:(h + 1) * t_new, :], 0.0)
        return acc

    for r in range(PG):
        sbuf[g * PG + r] = _sort_key(scores(_dot(qis, pages[r][0, 0].astype(BF16))))

    @pl.when(g == pl.num_programs(1) - 1)
    def _():
        trow = lax.broadcasted_iota(jnp.int32, (t_new, PAGE_SIZE), 0)
        lane = lax.broadcasted_iota(jnp.int32, (t_new, PAGE_SIZE), 1)
        sbuf[n_pages] = _sort_key(jnp.where(lane <= trow, scores(_dot_t(qis, kin_ref[0])), NEG_INF))

        def count(pred):
            acc = jnp.zeros((t_new, PAGE_SIZE), jnp.int32)
            for blk in range(n_pages + 1):
                acc = acc + jnp.where(pred(sbuf[blk], blk * PAGE_SIZE + lane), 1, 0)
            return jnp.sum(acc, axis=-1, keepdims=True)

        def bisect(i, carry):
            t, cnt_t = carry
            cand = t + lax.shift_left(jnp.int32(1), 31 - i)
            cnt = count(lambda blk, idx: blk >= cand)
            ok = cnt >= n_keep
            return jnp.where(ok, cand, t), jnp.where(ok, cnt, cnt_t)

        t, cnt_t = lax.fori_loop(0, 32, bisect, (jnp.full((t_new, 1), INT_MIN, jnp.int32),
                                                 jnp.full((t_new, 1), 2 ** 30, jnp.int32)))
        tied = jnp.logical_and(cnt_t > n_keep, t > KEY_NEG_INF)
        any_tied = jnp.max(jnp.where(tied, 1, 0)) > 0

        def tie_cut():
            need = n_keep - count(lambda blk, idx: blk > t)

            def step(i, x):
                cand = x + lax.shift_left(jnp.int32(1), 14 - i)
                cnt = count(lambda blk, idx: jnp.logical_and(blk == t, idx < cand))
                return jnp.where(cnt < need, cand, x)

            return lax.fori_loop(0, 15, step, jnp.zeros((t_new, 1), jnp.int32))

        jcut = lax.cond(any_tied, tie_cut, lambda: jnp.full((t_new, 1), 2 ** 30, jnp.int32))
        for blk in range(n_pages + 1):
            key = sbuf[blk]
            idx = blk * PAGE_SIZE + lane
            m = jnp.where(key > t, 0.0, jnp.where(key == t, jnp.where(idx <= jcut, 0.0, NEG_INF), NEG_INF))
            mask_ref[0, blk] = jnp.where(key > KEY_NEG_INF, m, NEG_INF)


def _dsa_select(page_table, qis, wi, ki_new, pool_ki, li, n_keep):
    bd_, n_pages = page_table.shape
    t_new = wi.shape[1]
    page_spec = lambda r: pl.BlockSpec((1, 1, D_IDX, PAGE_SIZE),
                                       lambda b, g, pt, r=r: (li, pt[b, g * PG + r], 0, 0))
    grid_spec = pltpu.PrefetchScalarGridSpec(
        num_scalar_prefetch=1,
        grid=(bd_, n_pages // PG),
        in_specs=[pl.BlockSpec((1, H_IDX * t_new, D_IDX), lambda b, g, pt: (b, 0, 0)),
                  pl.BlockSpec((1, t_new, H_IDX), lambda b, g, pt: (b, 0, 0)),
                  pl.BlockSpec((1, PAGE_SIZE, D_IDX), lambda b, g, pt: (b, 0, 0))]
                 + [page_spec(r) for r in range(PG)],
        out_specs=pl.BlockSpec((1, n_pages + 1, t_new, PAGE_SIZE), lambda b, g, pt: (b, 0, 0, 0)),
        scratch_shapes=[pltpu.VMEM((n_pages + 1, t_new, PAGE_SIZE), jnp.int32)],
    )
    return pl.pallas_call(
        functools.partial(_dsa_select_kernel, n_keep, n_pages),
        grid_spec=grid_spec,
        out_shape=jax.ShapeDtypeStruct((bd_, n_pages + 1, t_new, PAGE_SIZE), F32),
        compiler_params=_cparams(("parallel", "arbitrary")),
        name="dsa_sample_select",
    )(page_table, qis, wi, ki_new, *([pool_ki] * PG))


def _dsa_attend_kernel(n_pages, pt_ref, qbd_ref, mask_ref, bias_ref, kn_ref, vn_ref, *rest):
    kpages, vpages = rest[:PG], rest[PG:2 * PG]
    o_ref, lbuf, lnew, mx_ref, l_ref, acc_ref, stage = rest[2 * PG:]
    ph = pl.program_id(1)
    g = pl.program_id(2)
    ng = pl.num_programs(2)
    t_new = mask_ref.shape[2]
    rows = H_A * t_new
    qbd = qbd_ref[0]

    def fold(x):
        return [x[:, i * LANE:(i + 1) * LANE] for i in range(x.shape[1] // LANE)]

    def head_tile(m):
        return jnp.tile(m, (H_A, 1))

    @pl.when(ph == 0)
    def _():
        @pl.when(g == 0)
        def _():
            mx_ref[...] = jnp.full((rows, LANE), NEG_INF, F32)

        for r in range(PG):
            stage[:, r * PAGE_SIZE:(r + 1) * PAGE_SIZE] = kpages[r][0, 0].astype(BF16)
        s = _dot(qbd, stage[...])
        parts = []
        for r in range(PG):
            part = s[:, r * LANE:(r + 1) * LANE] + head_tile(mask_ref[0, g * PG + r])
            if r == PG - 1:
                part = part + jnp.where(g == ng - 1, 1.0, 0.0) * bias_ref[0]
            parts.append(part)
        lbuf[g] = jnp.concatenate(parts, axis=-1)
        mx = mx_ref[...]
        for part in parts:
            mx = jnp.maximum(mx, part)
        mx_ref[...] = mx

        @pl.when(g == ng - 1)
        def _():
            sn = _dot_t(qbd, kn_ref[0]) + head_tile(mask_ref[0, n_pages]) + bias_ref[1]
            lnew[...] = sn
            mx_ref[...] = jnp.maximum(mx_ref[...], sn)

    @pl.when(ph == 1)
    def _():
        m = jnp.max(mx_ref[...], axis=-1, keepdims=True)

        @pl.when(g == 0)
        def _():
            en = jnp.exp(lnew[...] - m)
            l_ref[...] = en
            acc_ref[...] = _dot(en.astype(BF16), vn_ref[0])

        for r in range(PG):
            stage[:, r * PAGE_SIZE:(r + 1) * PAGE_SIZE] = vpages[r][0, 0].astype(BF16)
        e = jnp.exp(lbuf[g] - m)
        lsum = l_ref[...]
        for part in fold(e):
            lsum = lsum + part
        l_ref[...] = lsum
        acc_ref[...] += _dot_t(e.astype(BF16), stage[...])

        @pl.when(g == ng - 1)
        def _():
            o = acc_ref[...] / jnp.sum(l_ref[...], axis=-1, keepdims=True)
            lo, _ = _half_masks(t_new)
            outs = []
            for p in range(H_A // 2):
                ls = slice(p * LANE, (p + 1) * LANE)
                outs.append(jnp.where(lo, o[2 * p * t_new:(2 * p + 1) * t_new, ls],
                                      o[(2 * p + 1) * t_new:(2 * p + 2) * t_new, ls]))
            o_ref[0] = jnp.concatenate(outs, axis=-1).astype(BF16)


def _dsa_attend(page_table, qbd, mask, bias, k_new, v_new, pool_k, pool_v, li):
    bd_, n_pages = page_table.shape
    rows = qbd.shape[1]
    t_new = rows // H_A
    ng = n_pages // PG
    kspec = lambda r: pl.BlockSpec(
        (1, 1, W_TOK, PAGE_SIZE),
        lambda b, ph, g, pt, r=r: (li, pt[b, jnp.where(ph == 0, g, ng - 1) * PG + r], 0, 0))
    vspec = lambda r: pl.BlockSpec(
        (1, 1, W_TOK, PAGE_SIZE),
        lambda b, ph, g, pt, r=r: (li, pt[b, jnp.where(ph == 0, 0, g) * PG + r], 0, 0))
    per_seq = lambda shape: pl.BlockSpec((1,) + shape, lambda b, ph, g, pt: (b,) + (0,) * len(shape))
    grid_spec = pltpu.PrefetchScalarGridSpec(
        num_scalar_prefetch=1,
        grid=(bd_, 2, ng),
        in_specs=[per_seq((rows, W_TOK)), per_seq((n_pages + 1, t_new, PAGE_SIZE)),
                  pl.BlockSpec((2, rows, PAGE_SIZE), lambda b, ph, g, pt: (0, 0, 0)),
                  per_seq((PAGE_SIZE, W_TOK)), per_seq((PAGE_SIZE, W_TOK))]
                 + [kspec(r) for r in range(PG)] + [vspec(r) for r in range(PG)],
        out_specs=per_seq((t_new, W_TOK)),
        scratch_shapes=[pltpu.VMEM((ng, rows, PG * PAGE_SIZE), F32), pltpu.VMEM((rows, LANE), F32),
                        pltpu.VMEM((rows, LANE), F32), pltpu.VMEM((rows, LANE), F32),
                        pltpu.VMEM((rows, W_TOK), F32), pltpu.VMEM((W_TOK, PG * PAGE_SIZE), BF16)],
    )
    return pl.pallas_call(
        functools.partial(_dsa_attend_kernel, n_pages),
        grid_spec=grid_spec,
        out_shape=jax.ShapeDtypeStruct((bd_, t_new, W_TOK), BF16),
        compiler_params=_cparams(("parallel", "arbitrary", "arbitrary")),
        name="dsa_sample_attend",
    )(page_table, qbd, mask, bias, k_new, v_new, *([pool_k] * PG), *([pool_v] * PG))


def _dsa_sample(outs, bias_s, pool_k, pool_v, pool_ki, li, page_table):
    kf, vf, kif, qb, kb, vb, qib, ki2, wis, qmb = outs
    bd_, n_pages = page_table.shape
    t_new = kf.shape[1] // bd_
    n_keep = min(TOPK_MAX, (n_pages * PAGE_SIZE + t_new) // 4)
    qis = qib.reshape(bd_, t_new, H_IDX, D_IDX).transpose(0, 2, 1, 3).reshape(bd_, H_IDX * t_new, D_IDX)
    pad_rows = lambda a: jnp.pad(a.reshape(bd_, t_new, a.shape[-1]), ((0, 0), (0, PAGE_SIZE - t_new), (0, 0)))
    mask = _dsa_select(page_table, qis, wis.reshape(bd_, t_new, H_IDX), pad_rows(ki2[..., :D_IDX]),
                       jnp.swapaxes(pool_ki, 2, 3), li, n_keep)
    q4 = qb.reshape(bd_, t_new, H_A, HEAD_DIM)
    qbd = jnp.einsum("bthd,hg->bhtgd", q4, jnp.eye(H_A, dtype=q4.dtype)).reshape(bd_, H_A * t_new, W_TOK)
    page_t = lambda a: jnp.transpose(a, (0, 1, 3, 4, 2)).reshape(a.shape[:2] + (W_TOK, PAGE_SIZE))
    return _dsa_attend(page_table, qbd, mask, bias_s, pad_rows(kb), pad_rows(vb), page_t(pool_k), page_t(pool_v), li)


def _trunk(x, p, pw, s5w, mem_k, mem_v, attend, conv_state, ssm_state, tm, flat):
    b, t, _ = x.shape
    shp = (1, b * t) if flat else (b, t)
    fl = lambda a: a.reshape(shp + a.shape[2:])
    unfl = lambda a: a.reshape((b, t) + a.shape[2:])
    bdm = pw["bd"][:W_MEM, :W_MEM]
    ks, vs, kis, convs, res, ims = [], [], [], [], [], []
    for l in range(DEPTH):
        kind, li = l % N_MIXERS, l // N_MIXERS
        if kind == 0:
            outs = _in_a(fl(x), pw["norm_mix"][l], pw["w_in_a"][li], pw["bd"], pw["qn_a"][li], pw["kn_a"][li],
                         pw["qn_mem"][l], tm)
            mix = unfl(attend(li, outs))
            ks.append(outs[0].reshape(b, t, H_A, HEAD_DIM))
            vs.append(outs[1].reshape(b, t, H_A, HEAD_DIM))
            kis.append(unfl(outs[2]))
            qm = unfl(outs[9])
        elif kind == 1:
            mix, qm, nst = _in_b(x, pw["norm_mix"][l], pw["w_in_b"][li], bdm, pw["qn_mem"][l], p["conv_w"][li],
                                 conv_state(li), min(tm, t))
            convs.append(nst)
        else:
            mix, qm, hout = _s5(x, pw["norm_mix"][l], pw["w_in_c"][li], bdm, pw["qn_mem"][l], *s5w[li],
                                ssm_state(li), min(tm // 2, t))
            res.append(hout[:, 0].reshape(b, SSM_G, SSM_P))
            ims.append(hout[:, 1].reshape(b, SSM_G, SSM_P))
        mo = _mem_attn(qm, mem_k[l], mem_v[l], min(tm, t))
        x = unfl(_post(fl(x), fl(mix), fl(mo), pw["w_out"][l], pw["norm_mlp"][l], pw["w_up"][l], pw["w_down"][l], tm))
    return x, jnp.stack(ks), jnp.stack(vs), jnp.stack(kis), jnp.stack(convs), jnp.stack(res), jnp.stack(ims)


def kernel(x_prompt, x_sample, cache_k, cache_v, cache_kidx, state_conv, state_ssm_re, state_ssm_im,
           cache_mem_k, cache_mem_v, page_table, mem_prompt, norm_mix, norm_mem, norm_mlp,
           w_in_a, w_in_b, w_in_c, w_out, qn_a, kn_a, rel_bias, conv_w, ssm_a_re, ssm_a_im, ssm_log_dt,
           ssm_b_re, ssm_b_im, ssm_c_re, ssm_c_im, ssm_d, w_glu, b_glu, w_mem_kv, qn_mem, kn_mem, w_up, w_down):
    p = dict(norm_mix=norm_mix, norm_mem=norm_mem, norm_mlp=norm_mlp, w_in_a=w_in_a, w_in_b=w_in_b,
             w_in_c=w_in_c, w_out=w_out, qn_a=qn_a, kn_a=kn_a, conv_w=conv_w, ssm_a_re=ssm_a_re,
             ssm_a_im=ssm_a_im, ssm_log_dt=ssm_log_dt, ssm_b_re=ssm_b_re, ssm_b_im=ssm_b_im, ssm_c_re=ssm_c_re,
             ssm_c_im=ssm_c_im, ssm_d=ssm_d, w_glu=w_glu, b_glu=b_glu, w_mem_kv=w_mem_kv, qn_mem=qn_mem,
             kn_mem=kn_mem, w_up=w_up, w_down=w_down)
    pw = _prep_weights(p)
    n_c = ssm_a_re.shape[0]
    s5w = [_s5_prep(p, li) for li in range(n_c)]
    bp, sp, _ = x_prompt.shape
    bd_, td, _ = x_sample.shape

    bias_p = _bias_tiles(rel_bias, jnp.asarray(_prompt_buckets()))
    pmk, pmv, pmk_b, pmv_b = _mem_kv(mem_prompt, pw["norm_mem"], pw["w_mem_kv"], pw["bd"][:W_MEM, :W_MEM],
                                     pw["kn_mem"])

    def attend_prompt(li, outs):
        kf, vf, kif, qb, kb, vb, qib, ki2, wis, qmb = outs
        return _dsa_prompt(qib, jnp.swapaxes(wis, 1, 2), ki2, qb, kb, vb, bias_p, min(TOPK_MAX, sp // 4))

    y_prompt, pk, pv, pki, pconv, pre, pim = _trunk(
        x_prompt, p, pw, s5w, pmk_b, pmv_b, attend_prompt,
        lambda li: jnp.zeros((bp, CONV_W - 1, W_TOK), F32),
        lambda li: jnp.zeros((bp, 2, SSM_N), F32), 512, False)

    bias_s = _bias_tiles(rel_bias, jnp.asarray(_sample_buckets(td)))
    bias_s = jnp.swapaxes(bias_s, 0, 1).reshape(2, H_A * td, PAGE_SIZE)
    mem_t = lambda a: jnp.transpose(a, (0, 1, 3, 4, 2)).reshape(DEPTH, bd_, W_MEM, N_MEM).astype(BF16)
    smk = mem_t(cache_mem_k)
    smv = mem_t(cache_mem_v)

    def attend_sample(li, outs):
        return _dsa_sample(outs, bias_s, cache_k, cache_v, cache_kidx, li, page_table)

    y_sample, sk, sv, ski, sconv, sre, sim = _trunk(
        x_sample, p, pw, s5w, smk, smv, attend_sample,
        lambda li: state_conv[li],
        lambda li: jnp.stack([state_ssm_re[li].reshape(bd_, SSM_N), state_ssm_im[li].reshape(bd_, SSM_N)], axis=1),
        bd_ * td, True)

    mem_out = lambda a: jnp.transpose(a.reshape(DEPTH, bp, H_MEM, HEAD_DIM, N_MEM), (0, 1, 4, 2, 3))
    return (y_prompt, y_sample, pk, pv, pki, pconv, pre, pim, mem_out(pmk), mem_out(pmv),
            sk, sv, ski, sconv, sre, sim)
```

```python
import functools
import math

import numpy as np
import jax
import jax.numpy as jnp
from jax import lax
from jax.experimental import pallas as pl
from jax.experimental.pallas import tpu as pltpu

D_MODEL = 1024
HEAD_DIM = 64
W_TOK = 768
W_MEM = 256
H_A = 12
H_MEM = 4
H_IDX = 8
D_IDX = 64
TOPK_MAX = 256
N_BUCKETS = 32
MAX_DISTANCE = 128
CONV_W = 3
SSM_GROUP = 16
SSM_G = 48
SSM_P = 64
SSM_N = SSM_G * SSM_P
D_FF = 4096
N_MEM = 256
PAGE_SIZE = 128
DEPTH = 4
N_MIXERS = 3
EPS = 1e-6
ATTN_SCALE = HEAD_DIM ** -0.5
IDX_SCALE = (H_IDX * D_IDX) ** -0.5

LANE = 128
QB = 256
VMEM_LIMIT = 56 * 1024 * 1024

BF16 = jnp.bfloat16
F32 = jnp.float32
NEG_INF = float("-inf")
INT_MIN = -2 ** 31
KEY_NEG_INF = int(np.array([0xFF800000], np.uint32).view(np.int32)[0]) ^ 0x7FFFFFFF


def _cparams(sem):
    return pltpu.CompilerParams(dimension_semantics=sem, vmem_limit_bytes=VMEM_LIMIT)


def _const_spec(shape):
    nd = len(shape)
    return pl.BlockSpec(shape, lambda *_: (0,) * nd, pipeline_mode=pl.Buffered(1))


def _dot(a, b):
    return jnp.dot(a, b, preferred_element_type=F32)


def _dot_t(a, b):
    return lax.dot_general(a, b, (((1,), (1,)), ((), ())), preferred_element_type=F32)


def _rms(x, g):
    ms = jnp.mean(x * x, axis=-1, keepdims=True)
    return x * lax.rsqrt(ms + EPS) * g


def _head_rms(x, g_tiled, bd):
    x2 = x * x
    hi = x2.astype(BF16)
    lo = (x2 - hi.astype(F32)).astype(BF16)
    b2 = bd[:LANE, :LANE]
    ss = jnp.concatenate([_dot(hi[:, i:i + LANE], b2) + _dot(lo[:, i:i + LANE], b2)
                          for i in range(0, x.shape[1], LANE)], axis=-1)
    return x * lax.rsqrt(ss * (1.0 / HEAD_DIM) + EPS) * g_tiled


def _half_masks(rows):
    lane = lax.broadcasted_iota(jnp.int32, (rows, LANE), 1)
    return lane < HEAD_DIM, lane >= HEAD_DIM


A_Q, A_K, A_V, A_QI, A_QM, A_KI, A_WI, A_END = 0, 768, 1536, 2304, 2816, 3072, 3136, 3200


def _in_a_kernel(x_ref, g_ref, w_ref, bd_ref, qn_ref, kn_ref, mn_ref,
                 k_ref, v_ref, ki_ref, qb_ref, kb_ref, vb_ref, qib_ref, ki2_ref, wi_ref, qm_ref):
    h = _rms(x_ref[0], g_ref[...]).astype(BF16)
    proj = _dot(h, w_ref[...])
    bd = bd_ref[...]
    q = _head_rms(proj[:, A_Q:A_K], qn_ref[...], bd) * ATTN_SCALE
    k = _head_rms(proj[:, A_K:A_V], kn_ref[...], bd)
    v = proj[:, A_V:A_QI]
    ki = proj[:, A_KI:A_KI + D_IDX]
    qm = _head_rms(proj[:, A_QM:A_KI], mn_ref[...], bd[:W_MEM, :W_MEM]) * ATTN_SCALE
    k_ref[0] = k
    v_ref[0] = v
    ki_ref[0] = ki
    qb_ref[0] = q.astype(BF16)
    kb_ref[0] = k.astype(BF16)
    vb_ref[0] = v.astype(BF16)
    qib_ref[0] = proj[:, A_QI:A_QM].astype(BF16)
    ki2_ref[0] = jnp.concatenate([ki, ki], axis=-1).astype(BF16)
    wi_ref[0] = proj[:, A_WI:A_WI + H_IDX] * IDX_SCALE
    qm_ref[0] = qm.astype(BF16)


def _in_a(x, g, w, bd, qn, kn, mn, tm):
    b, t, _ = x.shape
    tok = lambda w_, dt: jax.ShapeDtypeStruct((b, t, w_), dt)
    tspec = lambda w_: pl.BlockSpec((1, tm, w_), lambda i, j: (i, j, 0))
    widths = [(W_TOK, F32), (W_TOK, F32), (D_IDX, F32), (W_TOK, BF16), (W_TOK, BF16), (W_TOK, BF16),
              (H_IDX * D_IDX, BF16), (2 * D_IDX, BF16), (H_IDX, F32), (W_MEM, BF16)]
    return pl.pallas_call(
        _in_a_kernel,
        grid=(b, t // tm),
        in_specs=[tspec(D_MODEL), _const_spec((1, D_MODEL)), _const_spec((D_MODEL, A_END)),
                  _const_spec((W_TOK, W_TOK)), _const_spec((1, W_TOK)), _const_spec((1, W_TOK)),
                  _const_spec((1, W_MEM))],
        out_specs=[tspec(w_) for w_, _ in widths],
        out_shape=[tok(w_, dt) for w_, dt in widths],
        compiler_params=_cparams(("parallel", "parallel")),
        name="in_proj_dsa",
    )(x, g, w, bd, qn, kn, mn)


def _in_b_kernel(x_ref, g_ref, w_ref, bd_ref, mn_ref, cw_ref, st_ref,
                 mix_ref, qm_ref, nst_ref, ubuf):
    j = pl.program_id(1)
    tm = x_ref.shape[1]

    @pl.when(j == 0)
    def _():
        ubuf[6:8, :] = st_ref[0]

    h = _rms(x_ref[0], g_ref[...]).astype(BF16)
    proj = _dot(h, w_ref[...])
    u = proj[:, 0:W_TOK]
    gb = proj[:, W_TOK:2 * W_TOK]
    gc = proj[:, 2 * W_TOK:3 * W_TOK]
    ubuf[8:8 + tm, :] = gc * u
    y = cw_ref[0:1, :] * ubuf[6:6 + tm, :]
    y = y + cw_ref[1:2, :] * ubuf[7:7 + tm, :]
    y = y + cw_ref[2:3, :] * ubuf[8:8 + tm, :]
    mix_ref[0] = (gb * y).astype(BF16)
    last = ubuf[6 + tm:8 + tm, :]
    nst_ref[0] = last
    ubuf[6:8, :] = last
    qm = _head_rms(proj[:, 3 * W_TOK:], mn_ref[...], bd_ref[...]) * ATTN_SCALE
    qm_ref[0] = qm.astype(BF16)


def _in_b(x, g, w, bd, mn, cw, state, tm):
    b, t, _ = x.shape
    tspec = lambda w_: pl.BlockSpec((1, tm, w_), lambda i, j: (i, j, 0))
    return pl.pallas_call(
        _in_b_kernel,
        grid=(b, t // tm),
        in_specs=[tspec(D_MODEL), _const_spec((1, D_MODEL)), _const_spec((D_MODEL, 3 * W_TOK + W_MEM)),
                  _const_spec((W_MEM, W_MEM)), _const_spec((1, W_MEM)), _const_spec((CONV_W, W_TOK)),
                  pl.BlockSpec((1, CONV_W - 1, W_TOK), lambda i, j: (i, 0, 0))],
        out_specs=[tspec(W_TOK), tspec(W_MEM), pl.BlockSpec((1, CONV_W - 1, W_TOK), lambda i, j: (i, 0, 0))],
        out_shape=[jax.ShapeDtypeStruct((b, t, W_TOK), BF16), jax.ShapeDtypeStruct((b, t, W_MEM), BF16),
                   jax.ShapeDtypeStruct((b, CONV_W - 1, W_TOK), F32)],
        scratch_shapes=[pltpu.VMEM((tm + 8, W_TOK), F32)],
        compiler_params=_cparams(("parallel", "arbitrary")),
        name="in_proj_conv",
    )(x, g, w, bd, mn, cw, state)


def _s5_disc_kernel(are_ref, aim_ref, ldt_ref, bre_ref, bim_ref, lam_ref, bbre_ref, bbim_ref):
    a_re = are_ref[...]
    a_im = aim_ref[...]
    dt = jnp.exp(ldt_ref[...])
    mag = jnp.exp(a_re * dt)
    l_re = mag * jnp.cos(a_im * dt)
    l_im = mag * jnp.sin(a_im * dt)
    x_re = l_re - 1.0
    den = a_re * a_re + a_im * a_im
    c_re = (x_re * a_re + l_im * a_im) / den
    c_im = (l_im * a_re - x_re * a_im) / den
    p_re, p_im = l_re, l_im
    for i in range(S5_STEPS):
        lam_ref[0, i] = p_re
        lam_ref[1, i] = p_im
        p_re, p_im = p_re * l_re - p_im * l_im, p_re * l_im + p_im * l_re
    b_re = bre_ref[...]
    b_im = bim_ref[...]
    bbre_ref[...] = c_re[:, None, :] * b_re - c_im[:, None, :] * b_im
    bbim_ref[...] = c_re[:, None, :] * b_im + c_im[:, None, :] * b_re


def _s5_disc(a_re, a_im, log_dt, b_re_t, b_im_t):
    g3 = jax.ShapeDtypeStruct((SSM_G, SSM_GROUP, SSM_P), F32)
    return pl.pallas_call(
        _s5_disc_kernel,
        out_shape=[jax.ShapeDtypeStruct((2, S5_STEPS, SSM_G, SSM_P), F32), g3, g3],
        name="s5_discretise",
    )(a_re, a_im, log_dt.reshape(SSM_G, 1), b_re_t, b_im_t)


S5_LANES = 512
S5_SEG = 8
S5_STEPS = 32
N_BT = SSM_N // QB
N_CT = W_TOK // QB


def _scan_rows(sre, sim, lam_ref, hout_ref, tt):
    for lc in range(SSM_N // S5_LANES):
        sl = slice(lc * S5_LANES, (lc + 1) * S5_LANES)
        l_re = lam_ref[0, 0:1, sl]
        l_im = lam_ref[1, 0:1, sl]

        def step(i, carry, sl=sl, l_re=l_re, l_im=l_im):
            h_re, h_im = carry
            n_re = l_re * h_re - l_im * h_im + sre[pl.ds(i, 1), sl]
            n_im = l_re * h_im + l_im * h_re + sim[pl.ds(i, 1), sl]
            sre[pl.ds(i, 1), sl] = n_re
            sim[pl.ds(i, 1), sl] = n_im
            return n_re, n_im

        h_re, h_im = lax.fori_loop(0, tt, step, (hout_ref[0, 0:1, sl], hout_ref[0, 1:2, sl]))
        hout_ref[0, 0:1, sl] = h_re
        hout_ref[0, 1:2, sl] = h_im


def _scan_segments(sre, sim, lam_ref, hout_ref):
    for lc in range(SSM_N // S5_LANES):
        sl = slice(lc * S5_LANES, (lc + 1) * S5_LANES)
        l_re = jnp.broadcast_to(lam_ref[0, 0:1, sl], (S5_SEG, S5_LANES))
        l_im = jnp.broadcast_to(lam_ref[1, 0:1, sl], (S5_SEG, S5_LANES))
        rows = lambda i: pl.ds(pl.multiple_of(i * S5_SEG, S5_SEG), S5_SEG)

        def step(i, carry, sl=sl, l_re=l_re, l_im=l_im):
            h_re, h_im = carry
            n_re = l_re * h_re - l_im * h_im + sre[rows(i), sl]
            n_im = l_re * h_im + l_im * h_re + sim[rows(i), sl]
            sre[rows(i), sl] = n_re
            sim[rows(i), sl] = n_im
            return n_re, n_im

        zero = jnp.zeros((S5_SEG, S5_LANES), F32)
        e_re, e_im = lax.fori_loop(0, S5_STEPS, step, (zero, zero))
        p_re = lam_ref[0, S5_STEPS - 1:S5_STEPS, sl]
        p_im = lam_ref[1, S5_STEPS - 1:S5_STEPS, sl]
        c_re = hout_ref[0, 0:1, sl]
        c_im = hout_ref[0, 1:2, sl]
        in_re, in_im = [], []
        for s in range(S5_SEG):
            in_re.append(c_re)
            in_im.append(c_im)
            c_re, c_im = (e_re[s:s + 1] + p_re * c_re - p_im * c_im, e_im[s:s + 1] + p_re * c_im + p_im * c_re)
        hout_ref[0, 0:1, sl] = c_re
        hout_ref[0, 1:2, sl] = c_im
        hin_re = jnp.concatenate(in_re, axis=0)
        hin_im = jnp.concatenate(in_im, axis=0)

        def fix(i, _, sl=sl, hin_re=hin_re, hin_im=hin_im):
            q_re = lam_ref[0, pl.ds(i, 1), sl]
            q_im = lam_ref[1, pl.ds(i, 1), sl]
            sre[rows(i), sl] += q_re * hin_re - q_im * hin_im
            sim[rows(i), sl] += q_re * hin_im + q_im * hin_re
            return 0

        lax.fori_loop(0, S5_STEPS, fix, 0)


def _s5_kernel(segmented, x_ref, g_ref, w_ref, bd_ref, mn_ref, bwre_ref, bwim_ref, lam_ref, perm_ref,
               cwre_ref, cwim_ref, d_ref, wglu_ref, bglu_ref, h0_ref,
               mix_ref, qm_ref, hout_ref, sre, sim):
    j = pl.program_id(1)
    tt = x_ref.shape[1]

    @pl.when(j == 0)
    def _():
        hout_ref[0] = h0_ref[0]

    x = x_ref[0]
    if segmented:
        hi = x.astype(BF16)
        r1 = x - hi.astype(F32)
        mid = r1.astype(BF16)
        lo = (r1 - mid.astype(F32)).astype(BF16)
        x = _dot(perm_ref[0], hi) + _dot(perm_ref[0], mid) + _dot(perm_ref[0], lo)
    h = _rms(x, g_ref[...]).astype(BF16)
    proj = _dot(h, w_ref[...])
    u = proj[:, :W_TOK]
    ub = u.astype(BF16)
    for jt in range(N_BT):
        uk = ub[:, LANE * (jt // 2):LANE * (jt // 2 + 1)]
        sre[:, QB * jt:QB * (jt + 1)] = _dot(uk, bwre_ref[jt])
        sim[:, QB * jt:QB * (jt + 1)] = _dot(uk, bwim_ref[jt])

    if segmented:
        _scan_segments(sre, sim, lam_ref, hout_ref)
    else:
        _scan_rows(sre, sim, lam_ref, hout_ref, tt)

    ys = []
    for jc in range(N_CT):
        ks = slice(jc * 4 * QB, (jc + 1) * 4 * QB)
        ys.append(_dot(sre[:, ks].astype(BF16), cwre_ref[jc]) - _dot(sim[:, ks].astype(BF16), cwim_ref[jc]))
    y = jnp.concatenate(ys, axis=-1) + d_ref[...] * u
    y = jax.nn.gelu(y)
    z = _dot(y.astype(BF16), wglu_ref[...]) + bglu_ref[...]
    mix = (y * (1.0 / (1.0 + jnp.exp(-z)))).astype(BF16)
    qm = (_head_rms(proj[:, W_TOK:], mn_ref[...], bd_ref[...]) * ATTN_SCALE).astype(BF16)
    if segmented:
        mix = _dot(perm_ref[1], mix).astype(BF16)
        qm = _dot(perm_ref[1], qm).astype(BF16)
    mix_ref[0] = mix
    qm_ref[0] = qm


def _s5(x, g, w, bd, mn, bwre, bwim, lam, cwre, cwim, d, wglu, bglu, h0, tt):
    b, t, _ = x.shape
    segmented = tt == S5_SEG * S5_STEPS
    r = np.arange(tt)
    back = (r % S5_STEPS)[:, None] * S5_SEG + (r // S5_STEPS)[:, None] == r[None, :]
    perm = jnp.asarray(np.stack([back.T, back]), BF16)
    tspec = lambda w_: pl.BlockSpec((1, tt, w_), lambda i, j: (i, j, 0))
    sspec = pl.BlockSpec((1, 2, SSM_N), lambda i, j: (i, 0, 0))
    return pl.pallas_call(
        functools.partial(_s5_kernel, segmented),
        grid=(b, t // tt),
        in_specs=[tspec(D_MODEL), _const_spec((1, D_MODEL)), _const_spec((D_MODEL, W_TOK + W_MEM)),
                  _const_spec((W_MEM, W_MEM)), _const_spec((1, W_MEM)),
                  _const_spec((N_BT, LANE, QB)), _const_spec((N_BT, LANE, QB)),
                  _const_spec((2, S5_STEPS, SSM_N)), _const_spec((2, tt, tt)),
                  _const_spec((N_CT, 4 * QB, QB)), _const_spec((N_CT, 4 * QB, QB)),
                  _const_spec((1, W_TOK)), _const_spec((W_TOK, W_TOK)), _const_spec((1, W_TOK)), sspec],
        out_specs=[tspec(W_TOK), tspec(W_MEM), sspec],
        out_shape=[jax.ShapeDtypeStruct((b, t, W_TOK), BF16), jax.ShapeDtypeStruct((b, t, W_MEM), BF16),
                   jax.ShapeDtypeStruct((b, 2, SSM_N), F32)],
        scratch_shapes=[pltpu.VMEM((tt, SSM_N), F32), pltpu.VMEM((tt, SSM_N), F32)],
        compiler_params=_cparams(("parallel", "arbitrary")),
        name="s5_mixer",
    )(x, g, w, bd, mn, bwre, bwim, lam, perm, cwre, cwim, d, wglu, bglu, h0)


def _s5_weights(b_bar_re, b_bar_im, c_re, c_im):
    eye = jnp.eye(SSM_G, dtype=F32)

    def b_tiles(bb):
        full = jnp.einsum("gcp,gh->gchp", bb, eye).reshape(W_TOK, SSM_N)
        return jnp.stack([full[LANE * (j // 2):LANE * (j // 2 + 1), QB * j:QB * (j + 1)]
                          for j in range(N_BT)]).astype(BF16)

    def c_tiles(cc):
        full = jnp.einsum("gcp,gh->gphc", cc, eye).reshape(SSM_N, W_TOK)
        return jnp.stack([full[4 * QB * j:4 * QB * (j + 1), QB * j:QB * (j + 1)]
                          for j in range(N_CT)]).astype(BF16)

    return b_tiles(b_bar_re), b_tiles(b_bar_im), c_tiles(c_re), c_tiles(c_im)


def _mem_kv_kernel(m_ref, g_ref, w_ref, bd_ref, kn_ref, k_ref, v_ref, kb_ref, vb_ref):
    h = _rms(m_ref[0], g_ref[0]).astype(BF16)
    kv = _dot(h, w_ref[0])
    k = _head_rms(kv[:, :W_MEM], kn_ref[0], bd_ref[...]).T
    v = kv[:, W_MEM:].T
    k_ref[0, 0] = k
    v_ref[0, 0] = v
    kb_ref[0, 0] = k.astype(BF16)
    vb_ref[0, 0] = v.astype(BF16)


def _mem_kv(mem, g, w, bd, kn):
    b = mem.shape[0]
    ospec = pl.BlockSpec((1, 1, N_MEM, W_MEM), lambda l, i: (l, i, 0, 0))
    osh = lambda dt: jax.ShapeDtypeStruct((DEPTH, b, N_MEM, W_MEM), dt)
    return pl.pallas_call(
        _mem_kv_kernel,
        grid=(DEPTH, b),
        in_specs=[pl.BlockSpec((1, N_MEM, D_MODEL), lambda l, i: (i, 0, 0)),
                  pl.BlockSpec((1, 1, D_MODEL), lambda l, i: (l, 0, 0)),
                  pl.BlockSpec((1, D_MODEL, 2 * W_MEM), lambda l, i: (l, 0, 0)),
                  _const_spec((W_MEM, W_MEM)),
                  pl.BlockSpec((1, 1, W_MEM), lambda l, i: (l, 0, 0))],
        out_specs=[ospec] * 4,
        out_shape=[osh(F32), osh(F32), osh(BF16), osh(BF16)],
        compiler_params=_cparams(("parallel", "parallel")),
        name="mem_kv",
    )(mem, g, w, bd, kn)


def _mem_attn_kernel(qm_ref, k_ref, v_ref, o_ref):
    tm = qm_ref.shape[1]
    lo, hi = _half_masks(tm)
    outs = []
    for p in range(H_MEM // 2):
        ls = slice(p * LANE, (p + 1) * LANE)
        qp = qm_ref[0, :, ls]
        kp = k_ref[0, ls, :]
        vp = v_ref[0, ls, :]
        halves = []
        for msk in (lo, hi):
            s = _dot(jnp.where(msk, qp, jnp.zeros_like(qp)), kp)
            e = jnp.exp(s - jnp.max(s, axis=-1, keepdims=True))
            o = _dot_t(e.astype(BF16), vp)
            halves.append(o / jnp.sum(e, axis=-1, keepdims=True))
        outs.append(jnp.where(lo, halves[0], halves[1]))
    o_ref[0] = jnp.concatenate(outs, axis=-1).astype(BF16)


def _mem_attn(qm, k, v, tm):
    b, t, _ = qm.shape
    kvspec = pl.BlockSpec((1, N_MEM, W_MEM), lambda i, j: (i, 0, 0))
    tspec = pl.BlockSpec((1, tm, W_MEM), lambda i, j: (i, j, 0))
    return pl.pallas_call(
        _mem_attn_kernel,
        grid=(b, t // tm),
        in_specs=[tspec, kvspec, kvspec],
        out_specs=tspec,
        out_shape=jax.ShapeDtypeStruct((b, t, W_MEM), BF16),
        compiler_params=_cparams(("parallel", "parallel")),
        name="mem_attn",
    )(qm, k, v)


FF_CHUNK = 1024


def _post_kernel(x_ref, mix_ref, mo_ref, wo_ref, g_ref, wu_ref, wd_ref, o_ref):
    o_ref[0] = x_ref[0] + _dot(mix_ref[0], wo_ref[:W_TOK, :]) + _dot(mo_ref[0], wo_ref[W_TOK:, :])
    hm = _rms(o_ref[0], g_ref[...]).astype(BF16)
    for f in range(D_FF // FF_CHUNK):
        fs = slice(f * FF_CHUNK, (f + 1) * FF_CHUNK)
        a = jnp.maximum(_dot(hm, wu_ref[:, fs]), 0.0)
        o_ref[0] += _dot((a * a).astype(BF16), wd_ref[fs, :])


def _post(x, mix, mo, wo, g, wu, wd, tm):
    b, t, _ = x.shape
    tspec = lambda w_: pl.BlockSpec((1, tm, w_), lambda i, j: (i, j, 0))
    return pl.pallas_call(
        _post_kernel,
        grid=(b, t // tm),
        in_specs=[tspec(D_MODEL), tspec(W_TOK), tspec(W_MEM), _const_spec((D_MODEL, D_MODEL)),
                  _const_spec((1, D_MODEL)), _const_spec((D_MODEL, D_FF)), _const_spec((D_FF, D_MODEL))],
        out_specs=tspec(D_MODEL),
        out_shape=jax.ShapeDtypeStruct((b, t, D_MODEL), F32),
        compiler_params=_cparams(("parallel", "parallel")),
        name="out_proj_mlp",
    )(x, mix, mo, wo, g, wu, wd)


def _bucket_table(n):
    d = np.arange(n)
    exact = N_BUCKETS // 2
    nf = np.maximum(d, 1).astype(np.float32)
    far = exact + (np.log(nf / np.float32(exact)) / np.float32(math.log(MAX_DISTANCE / exact))
                   * np.float32(N_BUCKETS - exact)).astype(np.int32)
    return np.where(d < exact, d, np.minimum(far, N_BUCKETS - 1)).astype(np.int32)


def _bias_kernel(rb_ref, bk_ref, o_ref):
    bk = bk_ref[...]
    for h in range(H_A):
        far = rb_ref[N_BUCKETS - 1, h]
        for i in range(bk.shape[0]):
            t = jnp.zeros(bk.shape[1:], F32)
            for b in range(N_BUCKETS - 1):
                t = jnp.where(bk[i] == b, rb_ref[b, h] - far, t)
            o_ref[h, i] = t


def _bias_tiles(rel_bias, buckets):
    return pl.pallas_call(
        _bias_kernel,
        in_specs=[pl.BlockSpec(memory_space=pltpu.SMEM), pl.BlockSpec(memory_space=pltpu.VMEM)],
        out_shape=jax.ShapeDtypeStruct((H_A,) + buckets.shape, F32),
        compiler_params=pltpu.CompilerParams(vmem_limit_bytes=VMEM_LIMIT),
        name="rel_bias_tiles",
    )(rel_bias, buckets)


def _prompt_buckets():
    table = _bucket_table(2 * QB)
    tq = np.arange(QB)[:, None]
    sk = np.arange(QB)[None, :]
    diag = table[np.maximum(tq - sk, 0)]
    off = table[QB + tq - sk]
    return np.stack([off, diag]).astype(np.int32)


def _key_to_float(key):
    bits = jnp.where(key < 0, key ^ jnp.int32(0x7FFFFFFF), key)
    return jnp.where(key < KEY_NEG_INF, NEG_INF, pltpu.bitcast(bits, F32))


def _dsa_prompt_kernel(n_keep, qi_ref, wit_ref, ki2_ref, q_ref, k_ref, v_ref, bias_ref, o_ref,
                       sbuf, mbuf, qh_ref, mx_ref, l_ref, acc_ref):
    j = pl.program_id(1)
    nc = j + 1
    lo, hi = _half_masks(QB)
    zero_b = jnp.zeros((QB, LANE), BF16)
    chunk = lambda c: pl.ds(pl.multiple_of(c * QB, QB), QB)

    qi = qi_ref[0]
    qim = [jnp.where(hi if h % 2 else lo, qi[:, LANE * (h // 2):LANE * (h // 2 + 1)], zero_b)
           for h in range(H_IDX)]
    krow = lax.broadcasted_iota(jnp.int32, (QB, QB), 0)
    qcol = lax.broadcasted_iota(jnp.int32, (QB, QB), 1)

    def scores(c):
        kc = ki2_ref[0, chunk(c), :]
        acc = jnp.zeros((QB, QB), F32)
        for h in range(H_IDX):
            acc = acc + wit_ref[0, h:h + 1, :] * jnp.maximum(_dot_t(kc, qim[h]), 0.0)
        return acc

    def score_chunk(c, _):
        sbuf[chunk(c), :] = scores(c)
        return 0

    lax.fori_loop(0, j, score_chunk, 0)
    sbuf[chunk(j), :] = jnp.where(krow <= qcol, scores(j), NEG_INF)

    def count(pred):
        def body(c, acc):
            hit = jnp.where(pred(sbuf[chunk(c), :], c * QB + krow), 1, 0)
            return acc + jnp.sum(hit.reshape(QB // 8, 8, QB), axis=0)
        acc = lax.fori_loop(0, nc, body, jnp.zeros((8, QB), jnp.int32))
        return jnp.sum(acc, axis=0, keepdims=True)

    def bisect(i, carry):
        t, cnt_t = carry
        cand = t + lax.shift_left(jnp.int32(1), 31 - i)
        cand_f = _key_to_float(cand)
        cnt = count(lambda blk, idx: blk >= cand_f)
        ok = cnt >= n_keep
        return jnp.where(ok, cand, t), jnp.where(ok, cnt, cnt_t)

    t, cnt_t = lax.fori_loop(0, 32, bisect, (jnp.full((1, QB), INT_MIN, jnp.int32),
                                             jnp.full((1, QB), 2 ** 30, jnp.int32)))
    t = _key_to_float(t)

    tied = jnp.logical_and(cnt_t > n_keep, t > NEG_INF)
    any_tied = jnp.max(jnp.where(tied, 1, 0)) > 0

    def tie_cut():
        need = n_keep - count(lambda blk, idx: blk > t)

        def step(i, x):
            cand = x + lax.shift_left(jnp.int32(1), 12 - i)
            cnt = count(lambda blk, idx: jnp.logical_and(blk == t, idx < cand))
            return jnp.where(cnt < need, cand, x)

        return lax.fori_loop(0, 13, step, jnp.zeros((1, QB), jnp.int32))

    jcut = lax.cond(any_tied, tie_cut, lambda: jnp.full((1, QB), 2 ** 30, jnp.int32))

    def mask_chunk(c, _):
        s = sbuf[chunk(c), :]
        idx = c * QB + krow
        m = jnp.where(s > t, 0.0, jnp.where(s == t, jnp.where(idx <= jcut, 0.0, NEG_INF), NEG_INF))
        m = jnp.where(s > NEG_INF, m, NEG_INF)
        mbuf[c] = m.T
        return 0

    lax.fori_loop(0, nc, mask_chunk, 0)

    for p in range(H_A // 2):
        qp = q_ref[0, :, p * LANE:(p + 1) * LANE]
        qh_ref[p, :QB, :] = jnp.where(lo, qp, zero_b)
        qh_ref[p, QB:, :] = jnp.where(hi, qp, zero_b)
    mx_ref[...] = jnp.full(mx_ref.shape, NEG_INF, F32)

    def logits(c, p, near):
        r = _dot_t(qh_ref[p], k_ref[0, chunk(c), p * LANE:(p + 1) * LANE])
        out = []
        for hh in range(2):
            s = r[hh * QB:(hh + 1) * QB] + mbuf[c]
            if near is not None:
                s = s + bias_ref[2 * p + hh, near]
            out.append(s)
        return out

    def attend(c, near):
        for p in range(H_A // 2):
            es, alphas = [], []
            for hh, s in enumerate(logits(c, p, near)):
                h = 2 * p + hh
                m_old = mx_ref[h]
                row_max = jnp.max(jnp.maximum(s[:, :LANE], s[:, LANE:]), axis=-1, keepdims=True)
                m_new = jnp.maximum(m_old, row_max)
                mx_ref[h] = m_new
                m_new = jnp.where(m_new == NEG_INF, 0.0, m_new)
                alpha = jnp.exp(m_old - m_new)
                e0 = jnp.exp(s[:, :LANE] - m_new)
                e1 = jnp.exp(s[:, LANE:] - m_new)
                l_ref[h] = alpha * l_ref[h] + (e0 + e1)
                es.append(jnp.concatenate([e0, e1], axis=-1).astype(BF16))
                alphas.append(alpha)
            r = _dot(jnp.concatenate(es, axis=0), v_ref[0, chunk(c), p * LANE:(p + 1) * LANE])
            acc_ref[2 * p] = alphas[0] * acc_ref[2 * p] + r[:QB]
            acc_ref[2 * p + 1] = alphas[1] * acc_ref[2 * p + 1] + r[QB:]

    l_ref[...] = jnp.zeros(l_ref.shape, F32)
    acc_ref[...] = jnp.zeros(acc_ref.shape, F32)

    def far(c, _):
        attend(c, None)
        return 0

    lax.fori_loop(0, jnp.maximum(j - 1, 0), far, 0)

    @pl.when(j >= 1)
    def _():
        attend(j - 1, 0)

    attend(j, 1)
    for p in range(H_A // 2):
        o = [acc_ref[2 * p + hh] / jnp.sum(l_ref[2 * p + hh], axis=-1, keepdims=True) for hh in range(2)]
        o_ref[0, :, p * LANE:(p + 1) * LANE] = jnp.where(lo, o[0], o[1]).astype(BF16)


def _dsa_prompt(qi, wit, ki2, q, k, v, bias, n_keep):
    b, t, _ = q.shape
    nq = t // QB
    qspec = lambda w_: pl.BlockSpec((1, QB, w_), lambda i, j: (i, j, 0))
    kspec = lambda w_: pl.BlockSpec((1, t, w_), lambda i, j: (i, 0, 0))
    stat = pltpu.VMEM((H_A, QB, LANE), F32)
    return pl.pallas_call(
        functools.partial(_dsa_prompt_kernel, n_keep),
        grid=(b, nq),
        in_specs=[qspec(H_IDX * D_IDX), pl.BlockSpec((1, H_IDX, QB), lambda i, j: (i, 0, j)),
                  kspec(2 * D_IDX), qspec(W_TOK), kspec(W_TOK), kspec(W_TOK),
                  _const_spec((H_A, 2, QB, QB))],
        out_specs=qspec(W_TOK),
        out_shape=jax.ShapeDtypeStruct((b, t, W_TOK), BF16),
        scratch_shapes=[pltpu.VMEM((t, QB), F32), pltpu.VMEM((nq, QB, QB), F32),
                        pltpu.VMEM((H_A // 2, 2 * QB, LANE), BF16), stat, stat, stat],
        compiler_params=_cparams(("parallel", "arbitrary")),
        name="dsa_prompt",
    )(qi, wit, ki2, q, k, v, bias)


def _prep_weights(p):
    seg = np.arange(W_TOK) // HEAD_DIM
    bd = jnp.asarray(seg[:, None] == seg[None, :], BF16)
    row = lambda a: a.reshape(a.shape[0], 1, a.shape[-1])
    tile = lambda a, n: row(jnp.tile(a, (1, n)))
    wa = p["w_in_a"]
    pad = jnp.zeros(wa.shape[:2] + (A_END - A_WI - H_IDX,), wa.dtype)
    wa = jnp.concatenate([wa[..., 0:2304], wa[..., 2304:2816], wa[..., 2888:3144], wa[..., 2816:2880],
                          wa[..., 2880:2888], pad], axis=-1)
    return {
        "bd": bd,
        "norm_mix": row(p["norm_mix"]), "norm_mem": row(p["norm_mem"]), "norm_mlp": row(p["norm_mlp"]),
        "w_in_a": wa.astype(BF16), "w_in_b": p["w_in_b"].astype(BF16), "w_in_c": p["w_in_c"].astype(BF16),
        "w_out": p["w_out"].astype(BF16), "w_up": p["w_up"].astype(BF16), "w_down": p["w_down"].astype(BF16),
        "w_mem_kv": p["w_mem_kv"].astype(BF16),
        "qn_a": tile(p["qn_a"], H_A), "kn_a": tile(p["kn_a"], H_A),
        "qn_mem": tile(p["qn_mem"], H_MEM), "kn_mem": tile(p["kn_mem"], H_MEM),
    }


def _s5_prep(p, li):
    lam, bb_re, bb_im = _s5_disc(p["ssm_a_re"][li], p["ssm_a_im"][li], p["ssm_log_dt"][li],
                                 jnp.swapaxes(p["ssm_b_re"][li], 1, 2), jnp.swapaxes(p["ssm_b_im"][li], 1, 2))
    bwre, bwim, cwre, cwim = _s5_weights(bb_re, bb_im, p["ssm_c_re"][li], p["ssm_c_im"][li])
    return (bwre, bwim, lam.reshape(2, S5_STEPS, SSM_N), cwre, cwim, p["ssm_d"][li].reshape(1, W_TOK),
            p["w_glu"][li].astype(BF16), p["b_glu"][li].reshape(1, W_TOK))


PG = 16


def _sample_buckets(t_new):
    table = _bucket_table(2 * PAGE_SIZE + t_new)
    t = np.arange(t_new)[:, None]
    lane = np.arange(PAGE_SIZE)[None, :]
    return np.stack([table[PAGE_SIZE + t - lane], table[np.maximum(t - lane, 0)]]).astype(np.int32)


def _dsa_scores_kernel(n_pages, pt_ref, qis_ref, wi_ref, kin_ref, *rest):
    pages, s_ref = rest[:PG], rest[PG]
    g = pl.program_id(1)
    t_new = wi_ref.shape[1]
    qis = qis_ref[0]
    wi = wi_ref[0]

    def scores(d):
        acc = jnp.zeros((t_new, PAGE_SIZE), F32)
        for h in range(H_IDX):
            acc = acc + wi[:, h:h + 1] * jnp.maximum(d[h * t_new:(h + 1) * t_new, :], 0.0)
        return acc

    for r in range(PG):
        s_ref[g * PG + r] = scores(_dot(qis, pages[r][0, 0].astype(BF16)))

    @pl.when(g == pl.num_programs(1) - 1)
    def _():
        trow = lax.broadcasted_iota(jnp.int32, (t_new, PAGE_SIZE), 0)
        lane = lax.broadcasted_iota(jnp.int32, (t_new, PAGE_SIZE), 1)
        s_ref[n_pages] = jnp.where(lane <= trow, scores(_dot_t(qis, kin_ref[0])), NEG_INF)


ROW_GROUP = 64


def _dsa_threshold_kernel(n_keep, s_ref, m_ref):
    nb, rows, _ = s_ref.shape
    rgs = min(ROW_GROUP, rows)
    lane = lax.broadcasted_iota(jnp.int32, (rgs, PAGE_SIZE), 1)

    for rg in range(rows // rgs):
        rs = slice(rg * rgs, (rg + 1) * rgs)

        def count(pred, rs=rs):
            acc = jnp.zeros((rgs, PAGE_SIZE), jnp.int32)
            for blk in range(nb):
                acc = acc + jnp.where(pred(s_ref[blk, rs, :], blk * PAGE_SIZE + lane), 1, 0)
            return jnp.sum(acc, axis=-1, keepdims=True)

        def bisect(i, carry, count=count):
            t, cnt_t = carry
            cand = t + lax.shift_left(jnp.int32(1), 31 - i)
            cand_f = _key_to_float(cand)
            cnt = count(lambda blk, idx: blk >= cand_f)
            ok = cnt >= n_keep
            return jnp.where(ok, cand, t), jnp.where(ok, cnt, cnt_t)

        t, cnt_t = lax.fori_loop(0, 32, bisect, (jnp.full((rgs, 1), INT_MIN, jnp.int32),
                                                 jnp.full((rgs, 1), 2 ** 30, jnp.int32)))
        t = _key_to_float(t)
        tied = jnp.logical_and(cnt_t > n_keep, t > NEG_INF)
        any_tied = jnp.max(jnp.where(tied, 1, 0)) > 0

        def tie_cut(t=t, count=count):
            need = n_keep - count(lambda blk, idx: blk > t)

            def step(i, x):
                cand = x + lax.shift_left(jnp.int32(1), 14 - i)
                cnt = count(lambda blk, idx: jnp.logical_and(blk == t, idx < cand))
                return jnp.where(cnt < need, cand, x)

            return lax.fori_loop(0, 15, step, jnp.zeros((rgs, 1), jnp.int32))

        jcut = lax.cond(any_tied, tie_cut, lambda: jnp.full((rgs, 1), 2 ** 30, jnp.int32))
        for blk in range(nb):
            s = s_ref[blk, rs, :]
            idx = blk * PAGE_SIZE + lane
            m = jnp.where(s > t, 0.0, jnp.where(s == t, jnp.where(idx <= jcut, 0.0, NEG_INF), NEG_INF))
            m_ref[blk, rs, :] = jnp.where(s > NEG_INF, m, NEG_INF)


def _dsa_threshold(scores, n_keep):
    return pl.pallas_call(
        functools.partial(_dsa_threshold_kernel, n_keep),
        out_shape=jax.ShapeDtypeStruct(scores.shape, F32),
        compiler_params=pltpu.CompilerParams(vmem_limit_bytes=VMEM_LIMIT),
        name="dsa_sample_threshold",
    )(scores)


def _dsa_scores(page_table, qis, wi, ki_new, pool_ki, li):
    bd_, n_pages = page_table.shape
    t_new = wi.shape[1]
    page_spec = lambda r: pl.BlockSpec((1, 1, D_IDX, PAGE_SIZE),
                                       lambda b, g, pt, r=r: (li, pt[b, g * PG + r], 0, 0))
    grid_spec = pltpu.PrefetchScalarGridSpec(
        num_scalar_prefetch=1,
        grid=(bd_, n_pages // PG),
        in_specs=[pl.BlockSpec((1, H_IDX * t_new, D_IDX), lambda b, g, pt: (b, 0, 0)),
                  pl.BlockSpec((1, t_new, H_IDX), lambda b, g, pt: (b, 0, 0)),
                  pl.BlockSpec((1, PAGE_SIZE, D_IDX), lambda b, g, pt: (b, 0, 0))]
                 + [page_spec(r) for r in range(PG)],
        out_specs=pl.BlockSpec((n_pages + 1, t_new, PAGE_SIZE), lambda b, g, pt: (0, b, 0)),
    )
    return pl.pallas_call(
        functools.partial(_dsa_scores_kernel, n_pages),
        grid_spec=grid_spec,
        out_shape=jax.ShapeDtypeStruct((n_pages + 1, bd_ * t_new, PAGE_SIZE), F32),
        compiler_params=_cparams(("parallel", "arbitrary")),
        name="dsa_sample_scores",
    )(page_table, qis, wi, ki_new, *([pool_ki] * PG))


def _dsa_attend_kernel(n_pages, pt_ref, qbd_ref, mask_ref, bias_ref, kn_ref, vn_ref, *rest):
    kpages, vpages = rest[:PG], rest[PG:2 * PG]
    o_ref, mx_ref, l_ref, acc_ref, kstage, vstage = rest[2 * PG:]
    g = pl.program_id(1)
    ng = pl.num_programs(1)
    t_new = mask_ref.shape[1]
    rows = H_A * t_new
    qbd = qbd_ref[0]

    def head_tile(m):
        return jnp.tile(m, (H_A, 1))

    def update(parts, pv):
        m_old = mx_ref[...]
        cm = parts[0]
        for part in parts[1:]:
            cm = jnp.maximum(cm, part)
        m_new = jnp.maximum(m_old, jnp.max(cm, axis=-1, keepdims=True))
        mx_ref[...] = m_new
        m_new = jnp.where(m_new == NEG_INF, 0.0, m_new)
        alpha = jnp.exp(m_old - m_new)
        es = [jnp.exp(part - m_new) for part in parts]
        lsum = alpha * l_ref[...]
        for e in es:
            lsum = lsum + e
        l_ref[...] = lsum
        acc_ref[...] = (jnp.tile(alpha, (1, W_TOK // LANE)) * acc_ref[...]
                        + pv(jnp.concatenate(es, axis=-1).astype(BF16)))

    @pl.when(g == 0)
    def _():
        mx_ref[...] = jnp.full((rows, LANE), NEG_INF, F32)
        l_ref[...] = jnp.zeros((rows, LANE), F32)
        acc_ref[...] = jnp.zeros((rows, W_TOK), F32)
        sn = _dot_t(qbd, kn_ref[0]) + head_tile(mask_ref[n_pages]) + bias_ref[1]
        update([sn], lambda e: _dot(e, vn_ref[0]))

    for r in range(PG):
        kstage[:, r * PAGE_SIZE:(r + 1) * PAGE_SIZE] = kpages[r][0, 0].astype(BF16)
        vstage[:, r * PAGE_SIZE:(r + 1) * PAGE_SIZE] = vpages[r][0, 0].astype(BF16)
    s = _dot(qbd, kstage[...])
    parts = []
    for r in range(PG):
        part = s[:, r * LANE:(r + 1) * LANE] + head_tile(mask_ref[g * PG + r])
        if r == PG - 1:
            part = part + jnp.where(g == ng - 1, 1.0, 0.0) * bias_ref[0]
        parts.append(part)
    update(parts, lambda e: _dot_t(e, vstage[...]))

    @pl.when(g == ng - 1)
    def _():
        o = acc_ref[...] / jnp.sum(l_ref[...], axis=-1, keepdims=True)
        lo, _ = _half_masks(t_new)
        outs = []
        for p in range(H_A // 2):
            ls = slice(p * LANE, (p + 1) * LANE)
            outs.append(jnp.where(lo, o[2 * p * t_new:(2 * p + 1) * t_new, ls],
                                  o[(2 * p + 1) * t_new:(2 * p + 2) * t_new, ls]))
        o_ref[0] = jnp.concatenate(outs, axis=-1).astype(BF16)


def _dsa_attend(page_table, qbd, mask, bias, k_new, v_new, pool_k, pool_v, li):
    bd_, n_pages = page_table.shape
    rows = qbd.shape[1]
    t_new = rows // H_A
    ng = n_pages // PG
    pspec = lambda r: pl.BlockSpec((1, 1, W_TOK, PAGE_SIZE), lambda b, g, pt, r=r: (li, pt[b, g * PG + r], 0, 0))
    per_seq = lambda shape: pl.BlockSpec((1,) + shape, lambda b, g, pt: (b,) + (0,) * len(shape))
    grid_spec = pltpu.PrefetchScalarGridSpec(
        num_scalar_prefetch=1,
        grid=(bd_, ng),
        in_specs=[per_seq((rows, W_TOK)),
                  pl.BlockSpec((n_pages + 1, t_new, PAGE_SIZE), lambda b, g, pt: (0, b, 0)),
                  pl.BlockSpec((2, rows, PAGE_SIZE), lambda b, g, pt: (0, 0, 0)),
                  per_seq((PAGE_SIZE, W_TOK)), per_seq((PAGE_SIZE, W_TOK))]
                 + [pspec(r) for r in range(PG)] * 2,
        out_specs=per_seq((t_new, W_TOK)),
        scratch_shapes=[pltpu.VMEM((rows, LANE), F32), pltpu.VMEM((rows, LANE), F32),
                        pltpu.VMEM((rows, W_TOK), F32), pltpu.VMEM((W_TOK, PG * PAGE_SIZE), BF16),
                        pltpu.VMEM((W_TOK, PG * PAGE_SIZE), BF16)],
    )
    return pl.pallas_call(
        functools.partial(_dsa_attend_kernel, n_pages),
        grid_spec=grid_spec,
        out_shape=jax.ShapeDtypeStruct((bd_, t_new, W_TOK), BF16),
        compiler_params=_cparams(("parallel", "arbitrary")),
        name="dsa_sample_attend",
    )(page_table, qbd, mask, bias, k_new, v_new, *([pool_k] * PG), *([pool_v] * PG))


def _dsa_sample(outs, bias_s, pool_k, pool_v, pool_ki, li, page_table):
    kf, vf, kif, qb, kb, vb, qib, ki2, wis, qmb = outs
    bd_, n_pages = page_table.shape
    t_new = kf.shape[1] // bd_
    n_keep = min(TOPK_MAX, (n_pages * PAGE_SIZE + t_new) // 4)
    qis = qib.reshape(bd_, t_new, H_IDX, D_IDX).transpose(0, 2, 1, 3).reshape(bd_, H_IDX * t_new, D_IDX)
    pad_rows = lambda a: jnp.pad(a.reshape(bd_, t_new, a.shape[-1]), ((0, 0), (0, PAGE_SIZE - t_new), (0, 0)))
    scores = _dsa_scores(page_table, qis, wis.reshape(bd_, t_new, H_IDX), pad_rows(ki2[..., :D_IDX]),
                         jnp.swapaxes(pool_ki, 2, 3), li)
    mask = _dsa_threshold(scores, n_keep)
    q4 = qb.reshape(bd_, t_new, H_A, HEAD_DIM)
    qbd = jnp.einsum("bthd,hg->bhtgd", q4, jnp.eye(H_A, dtype=q4.dtype)).reshape(bd_, H_A * t_new, W_TOK)
    page_t = lambda a: jnp.transpose(a, (0, 1, 3, 4, 2)).reshape(a.shape[:2] + (W_TOK, PAGE_SIZE))
    return _dsa_attend(page_table, qbd, mask, bias_s, pad_rows(kb), pad_rows(vb), page_t(pool_k), page_t(pool_v), li)


def _trunk(x, p, pw, s5w, mem_k, mem_v, attend, conv_state, ssm_state, tm, flat):
    b, t, _ = x.shape
    shp = (1, b * t) if flat else (b, t)
    fl = lambda a: a.reshape(shp + a.shape[2:])
    unfl = lambda a: a.reshape((b, t) + a.shape[2:])
    bdm = pw["bd"][:W_MEM, :W_MEM]
    ks, vs, kis, convs, res, ims = [], [], [], [], [], []
    for l in range(DEPTH):
        kind, li = l % N_MIXERS, l // N_MIXERS
        if kind == 0:
            outs = _in_a(fl(x), pw["norm_mix"][l], pw["w_in_a"][li], pw["bd"], pw["qn_a"][li], pw["kn_a"][li],
                         pw["qn_mem"][l], tm)
            mix = unfl(attend(li, outs))
            ks.append(outs[0].reshape(b, t, H_A, HEAD_DIM))
            vs.append(outs[1].reshape(b, t, H_A, HEAD_DIM))
            kis.append(unfl(outs[2]))
            qm = unfl(outs[9])
        elif kind == 1:
            mix, qm, nst = _in_b(x, pw["norm_mix"][l], pw["w_in_b"][li], bdm, pw["qn_mem"][l], p["conv_w"][li],
                                 conv_state(li), min(tm, t))
            convs.append(nst)
        else:
            mix, qm, hout = _s5(x, pw["norm_mix"][l], pw["w_in_c"][li], bdm, pw["qn_mem"][l], *s5w[li],
                                ssm_state(li), min(tm // 2, t))
            res.append(hout[:, 0].reshape(b, SSM_G, SSM_P))
            ims.append(hout[:, 1].reshape(b, SSM_G, SSM_P))
        mo = _mem_attn(qm, mem_k[l], mem_v[l], min(tm, t))
        x = unfl(_post(fl(x), fl(mix), fl(mo), pw["w_out"][l], pw["norm_mlp"][l], pw["w_up"][l], pw["w_down"][l], tm))
    return x, jnp.stack(ks), jnp.stack(vs), jnp.stack(kis), jnp.stack(convs), jnp.stack(res), jnp.stack(ims)


def kernel(x_prompt, x_sample, cache_k, cache_v, cache_kidx, state_conv, state_ssm_re, state_ssm_im,
           cache_mem_k, cache_mem_v, page_table, mem_prompt, norm_mix, norm_mem, norm_mlp,
           w_in_a, w_in_b, w_in_c, w_out, qn_a, kn_a, rel_bias, conv_w, ssm_a_re, ssm_a_im, ssm_log_dt,
           ssm_b_re, ssm_b_im, ssm_c_re, ssm_c_im, ssm_d, w_glu, b_glu, w_mem_kv, qn_mem, kn_mem, w_up, w_down):
    p = dict(norm_mix=norm_mix, norm_mem=norm_mem, norm_mlp=norm_mlp, w_in_a=w_in_a, w_in_b=w_in_b,
             w_in_c=w_in_c, w_out=w_out, qn_a=qn_a, kn_a=kn_a, conv_w=conv_w, ssm_a_re=ssm_a_re,
             ssm_a_im=ssm_a_im, ssm_log_dt=ssm_log_dt, ssm_b_re=ssm_b_re, ssm_b_im=ssm_b_im, ssm_c_re=ssm_c_re,
             ssm_c_im=ssm_c_im, ssm_d=ssm_d, w_glu=w_glu, b_glu=b_glu, w_mem_kv=w_mem_kv, qn_mem=qn_mem,
             kn_mem=kn_mem, w_up=w_up, w_down=w_down)
    pw = _prep_weights(p)
    n_c = ssm_a_re.shape[0]
    s5w = [_s5_prep(p, li) for li in range(n_c)]
    bp, sp, _ = x_prompt.shape
    bd_, td, _ = x_sample.shape

    bias_p = _bias_tiles(rel_bias, jnp.asarray(_prompt_buckets()))
    pmk, pmv, pmk_b, pmv_b = _mem_kv(mem_prompt, pw["norm_mem"], pw["w_mem_kv"], pw["bd"][:W_MEM, :W_MEM],
                                     pw["kn_mem"])

    def attend_prompt(li, outs):
        kf, vf, kif, qb, kb, vb, qib, ki2, wis, qmb = outs
        return _dsa_prompt(qib, jnp.swapaxes(wis, 1, 2), ki2, qb, kb, vb, bias_p, min(TOPK_MAX, sp // 4))

    y_prompt, pk, pv, pki, pconv, pre, pim = _trunk(
        x_prompt, p, pw, s5w, pmk_b, pmv_b, attend_prompt,
        lambda li: jnp.zeros((bp, CONV_W - 1, W_TOK), F32),
        lambda li: jnp.zeros((bp, 2, SSM_N), F32), 512, False)

    bias_s = _bias_tiles(rel_bias, jnp.asarray(_sample_buckets(td)))
    bias_s = jnp.swapaxes(bias_s, 0, 1).reshape(2, H_A * td, PAGE_SIZE)
    mem_t = lambda a: jnp.transpose(a, (0, 1, 3, 4, 2)).reshape(DEPTH, bd_, W_MEM, N_MEM).astype(BF16)
    smk = mem_t(cache_mem_k)
    smv = mem_t(cache_mem_v)

    def attend_sample(li, outs):
        return _dsa_sample(outs, bias_s, cache_k, cache_v, cache_kidx, li, page_table)

    y_sample, sk, sv, ski, sconv, sre, sim = _trunk(
        x_sample, p, pw, s5w, smk, smv, attend_sample,
        lambda li: state_conv[li],
        lambda li: jnp.stack([state_ssm_re[li].reshape(bd_, SSM_N), state_ssm_im[li].reshape(bd_, SSM_N)], axis=1),
        bd_ * td, True)

    mem_out = lambda a: jnp.transpose(a.reshape(DEPTH, bp, H_MEM, HEAD_DIM, N_MEM), (0, 1, 4, 2, 3))
    return (y_prompt, y_sample, pk, pv, pki, pconv, pre, pim, mem_out(pmk), mem_out(pmv),
            sk, sv, ski, sconv, sre, sim)
```

```python
import functools
import math

import numpy as np
import jax
import jax.numpy as jnp
from jax import lax
from jax.experimental import pallas as pl
from jax.experimental.pallas import tpu as pltpu

D_MODEL = 1024
HEAD_DIM = 64
W_TOK = 768
W_MEM = 256
H_A = 12
H_MEM = 4
H_IDX = 8
D_IDX = 64
TOPK_MAX = 256
N_BUCKETS = 32
MAX_DISTANCE = 128
CONV_W = 3
SSM_GROUP = 16
SSM_G = 48
SSM_P = 64
SSM_N = SSM_G * SSM_P
D_FF = 4096
N_MEM = 256
PAGE_SIZE = 128
DEPTH = 4
N_MIXERS = 3
EPS = 1e-6
ATTN_SCALE = HEAD_DIM ** -0.5
IDX_SCALE = (H_IDX * D_IDX) ** -0.5

LANE = 128
QB = 256
VMEM_LIMIT = 56 * 1024 * 1024

BF16 = jnp.bfloat16
F32 = jnp.float32
NEG_INF = float("-inf")
INT_MIN = -2 ** 31
KEY_NEG_INF = int(np.array([0xFF800000], np.uint32).view(np.int32)[0]) ^ 0x7FFFFFFF


def _cparams(sem):
    return pltpu.CompilerParams(dimension_semantics=sem, vmem_limit_bytes=VMEM_LIMIT)


def _const_spec(shape):
    nd = len(shape)
    return pl.BlockSpec(shape, lambda *_: (0,) * nd, pipeline_mode=pl.Buffered(1))


def _dot(a, b):
    return jnp.dot(a, b, preferred_element_type=F32)


def _dot_t(a, b):
    return lax.dot_general(a, b, (((1,), (1,)), ((), ())), preferred_element_type=F32)


def _rms(x, g):
    ms = jnp.mean(x * x, axis=-1, keepdims=True)
    return x * lax.rsqrt(ms + EPS) * g


def _head_rms(x, g_tiled, bd):
    x2 = x * x
    hi = x2.astype(BF16)
    lo = (x2 - hi.astype(F32)).astype(BF16)
    b2 = bd[:LANE, :LANE]
    ss = jnp.concatenate([_dot(hi[:, i:i + LANE], b2) + _dot(lo[:, i:i + LANE], b2)
                          for i in range(0, x.shape[1], LANE)], axis=-1)
    return x * lax.rsqrt(ss * (1.0 / HEAD_DIM) + EPS) * g_tiled


def _half_masks(rows):
    lane = lax.broadcasted_iota(jnp.int32, (rows, LANE), 1)
    return lane < HEAD_DIM, lane >= HEAD_DIM


A_Q, A_K, A_V, A_QI, A_QM, A_KI, A_WI, A_END = 0, 768, 1536, 2304, 2816, 3072, 3136, 3200


def _in_a_kernel(x_ref, g_ref, w_ref, bd_ref, qn_ref, kn_ref, mn_ref,
                 k_ref, v_ref, ki_ref, qb_ref, kb_ref, vb_ref, qib_ref, ki2_ref, wi_ref, qm_ref):
    h = _rms(x_ref[0], g_ref[...]).astype(BF16)
    proj = _dot(h, w_ref[...])
    bd = bd_ref[...]
    q = _head_rms(proj[:, A_Q:A_K], qn_ref[...], bd) * ATTN_SCALE
    k = _head_rms(proj[:, A_K:A_V], kn_ref[...], bd)
    v = proj[:, A_V:A_QI]
    ki = proj[:, A_KI:A_KI + D_IDX]
    qm = _head_rms(proj[:, A_QM:A_KI], mn_ref[...], bd[:W_MEM, :W_MEM]) * ATTN_SCALE
    k_ref[0] = k
    v_ref[0] = v
    ki_ref[0] = ki
    qb_ref[0] = q.astype(BF16)
    kb_ref[0] = k.astype(BF16)
    vb_ref[0] = v.astype(BF16)
    qib_ref[0] = proj[:, A_QI:A_QM].astype(BF16)
    ki2_ref[0] = jnp.concatenate([ki, ki], axis=-1).astype(BF16)
    wi_ref[0] = proj[:, A_WI:A_WI + H_IDX] * IDX_SCALE
    qm_ref[0] = qm.astype(BF16)


def _in_a(x, g, w, bd, qn, kn, mn, tm):
    b, t, _ = x.shape
    tok = lambda w_, dt: jax.ShapeDtypeStruct((b, t, w_), dt)
    tspec = lambda w_: pl.BlockSpec((1, tm, w_), lambda i, j: (i, j, 0))
    widths = [(W_TOK, F32), (W_TOK, F32), (D_IDX, F32), (W_TOK, BF16), (W_TOK, BF16), (W_TOK, BF16),
              (H_IDX * D_IDX, BF16), (2 * D_IDX, BF16), (H_IDX, F32), (W_MEM, BF16)]
    return pl.pallas_call(
        _in_a_kernel,
        grid=(b, t // tm),
        in_specs=[tspec(D_MODEL), _const_spec((1, D_MODEL)), _const_spec((D_MODEL, A_END)),
                  _const_spec((W_TOK, W_TOK)), _const_spec((1, W_TOK)), _const_spec((1, W_TOK)),
                  _const_spec((1, W_MEM))],
        out_specs=[tspec(w_) for w_, _ in widths],
        out_shape=[tok(w_, dt) for w_, dt in widths],
        compiler_params=_cparams(("parallel", "parallel")),
        name="in_proj_dsa",
    )(x, g, w, bd, qn, kn, mn)


def _in_b_kernel(x_ref, g_ref, w_ref, bd_ref, mn_ref, cw_ref, st_ref,
                 mix_ref, qm_ref, nst_ref, ubuf):
    j = pl.program_id(1)
    tm = x_ref.shape[1]

    @pl.when(j == 0)
    def _():
        ubuf[6:8, :] = st_ref[0]

    h = _rms(x_ref[0], g_ref[...]).astype(BF16)
    proj = _dot(h, w_ref[...])
    u = proj[:, 0:W_TOK]
    gb = proj[:, W_TOK:2 * W_TOK]
    gc = proj[:, 2 * W_TOK:3 * W_TOK]
    ubuf[8:8 + tm, :] = gc * u
    y = cw_ref[0:1, :] * ubuf[6:6 + tm, :]
    y = y + cw_ref[1:2, :] * ubuf[7:7 + tm, :]
    y = y + cw_ref[2:3, :] * ubuf[8:8 + tm, :]
    mix_ref[0] = (gb * y).astype(BF16)
    last = ubuf[6 + tm:8 + tm, :]
    nst_ref[0] = last
    ubuf[6:8, :] = last
    qm = _head_rms(proj[:, 3 * W_TOK:], mn_ref[...], bd_ref[...]) * ATTN_SCALE
    qm_ref[0] = qm.astype(BF16)


def _in_b(x, g, w, bd, mn, cw, state, tm):
    b, t, _ = x.shape
    tspec = lambda w_: pl.BlockSpec((1, tm, w_), lambda i, j: (i, j, 0))
    return pl.pallas_call(
        _in_b_kernel,
        grid=(b, t // tm),
        in_specs=[tspec(D_MODEL), _const_spec((1, D_MODEL)), _const_spec((D_MODEL, 3 * W_TOK + W_MEM)),
                  _const_spec((W_MEM, W_MEM)), _const_spec((1, W_MEM)), _const_spec((CONV_W, W_TOK)),
                  pl.BlockSpec((1, CONV_W - 1, W_TOK), lambda i, j: (i, 0, 0))],
        out_specs=[tspec(W_TOK), tspec(W_MEM), pl.BlockSpec((1, CONV_W - 1, W_TOK), lambda i, j: (i, 0, 0))],
        out_shape=[jax.ShapeDtypeStruct((b, t, W_TOK), BF16), jax.ShapeDtypeStruct((b, t, W_MEM), BF16),
                   jax.ShapeDtypeStruct((b, CONV_W - 1, W_TOK), F32)],
        scratch_shapes=[pltpu.VMEM((tm + 8, W_TOK), F32)],
        compiler_params=_cparams(("parallel", "arbitrary")),
        name="in_proj_conv",
    )(x, g, w, bd, mn, cw, state)


def _s5_disc_kernel(are_ref, aim_ref, ldt_ref, bre_ref, bim_ref, lam_ref, bbre_ref, bbim_ref):
    a_re = are_ref[...]
    a_im = aim_ref[...]
    dt = jnp.exp(ldt_ref[...])
    mag = jnp.exp(a_re * dt)
    l_re = mag * jnp.cos(a_im * dt)
    l_im = mag * jnp.sin(a_im * dt)
    x_re = l_re - 1.0
    den = a_re * a_re + a_im * a_im
    c_re = (x_re * a_re + l_im * a_im) / den
    c_im = (l_im * a_re - x_re * a_im) / den
    p_re, p_im = l_re, l_im
    for i in range(S5_STEPS):
        lam_ref[0, i] = p_re
        lam_ref[1, i] = p_im
        p_re, p_im = p_re * l_re - p_im * l_im, p_re * l_im + p_im * l_re
    b_re = bre_ref[...]
    b_im = bim_ref[...]
    bbre_ref[...] = c_re[:, None, :] * b_re - c_im[:, None, :] * b_im
    bbim_ref[...] = c_re[:, None, :] * b_im + c_im[:, None, :] * b_re


def _s5_disc(a_re, a_im, log_dt, b_re_t, b_im_t):
    g3 = jax.ShapeDtypeStruct((SSM_G, SSM_GROUP, SSM_P), F32)
    return pl.pallas_call(
        _s5_disc_kernel,
        out_shape=[jax.ShapeDtypeStruct((2, S5_STEPS, SSM_G, SSM_P), F32), g3, g3],
        name="s5_discretise",
    )(a_re, a_im, log_dt.reshape(SSM_G, 1), b_re_t, b_im_t)


S5_LANES = 512
S5_SEG = 8
S5_STEPS = 32
N_BT = SSM_N // QB
N_CT = W_TOK // QB


def _scan_rows(sre, sim, lam_ref, hout_ref, tt):
    for lc in range(SSM_N // S5_LANES):
        sl = slice(lc * S5_LANES, (lc + 1) * S5_LANES)
        l_re = lam_ref[0, 0:1, sl]
        l_im = lam_ref[1, 0:1, sl]

        def step(i, carry, sl=sl, l_re=l_re, l_im=l_im):
            h_re, h_im = carry
            n_re = l_re * h_re - l_im * h_im + sre[pl.ds(i, 1), sl]
            n_im = l_re * h_im + l_im * h_re + sim[pl.ds(i, 1), sl]
            sre[pl.ds(i, 1), sl] = n_re
            sim[pl.ds(i, 1), sl] = n_im
            return n_re, n_im

        h_re, h_im = lax.fori_loop(0, tt, step, (hout_ref[0, 0:1, sl], hout_ref[0, 1:2, sl]))
        hout_ref[0, 0:1, sl] = h_re
        hout_ref[0, 1:2, sl] = h_im


def _scan_segments(sre, sim, lam_ref, hout_ref):
    for lc in range(SSM_N // S5_LANES):
        sl = slice(lc * S5_LANES, (lc + 1) * S5_LANES)
        l_re = jnp.broadcast_to(lam_ref[0, 0:1, sl], (S5_SEG, S5_LANES))
        l_im = jnp.broadcast_to(lam_ref[1, 0:1, sl], (S5_SEG, S5_LANES))
        rows = lambda i: pl.ds(pl.multiple_of(i * S5_SEG, S5_SEG), S5_SEG)

        def step(i, carry, sl=sl, l_re=l_re, l_im=l_im):
            h_re, h_im = carry
            n_re = l_re * h_re - l_im * h_im + sre[rows(i), sl]
            n_im = l_re * h_im + l_im * h_re + sim[rows(i), sl]
            sre[rows(i), sl] = n_re
            sim[rows(i), sl] = n_im
            return n_re, n_im

        zero = jnp.zeros((S5_SEG, S5_LANES), F32)
        e_re, e_im = lax.fori_loop(0, S5_STEPS, step, (zero, zero))
        p_re = lam_ref[0, S5_STEPS - 1:S5_STEPS, sl]
        p_im = lam_ref[1, S5_STEPS - 1:S5_STEPS, sl]
        c_re = hout_ref[0, 0:1, sl]
        c_im = hout_ref[0, 1:2, sl]
        in_re, in_im = [], []
        for s in range(S5_SEG):
            in_re.append(c_re)
            in_im.append(c_im)
            c_re, c_im = (e_re[s:s + 1] + p_re * c_re - p_im * c_im, e_im[s:s + 1] + p_re * c_im + p_im * c_re)
        hout_ref[0, 0:1, sl] = c_re
        hout_ref[0, 1:2, sl] = c_im
        hin_re = jnp.concatenate(in_re, axis=0)
        hin_im = jnp.concatenate(in_im, axis=0)

        def fix(i, _, sl=sl, hin_re=hin_re, hin_im=hin_im):
            q_re = lam_ref[0, pl.ds(i, 1), sl]
            q_im = lam_ref[1, pl.ds(i, 1), sl]
            sre[rows(i), sl] += q_re * hin_re - q_im * hin_im
            sim[rows(i), sl] += q_re * hin_im + q_im * hin_re
            return 0

        lax.fori_loop(0, S5_STEPS, fix, 0)


def _s5_kernel(segmented, x_ref, g_ref, w_ref, bd_ref, mn_ref, bwre_ref, bwim_ref, lam_ref, perm_ref,
               cwre_ref, cwim_ref, d_ref, wglu_ref, bglu_ref, h0_ref,
               mix_ref, qm_ref, hout_ref, sre, sim):
    j = pl.program_id(1)
    tt = x_ref.shape[1]

    @pl.when(j == 0)
    def _():
        hout_ref[0] = h0_ref[0]

    x = x_ref[0]
    if segmented:
        hi = x.astype(BF16)
        r1 = x - hi.astype(F32)
        mid = r1.astype(BF16)
        lo = (r1 - mid.astype(F32)).astype(BF16)
        x = _dot(perm_ref[0], hi) + _dot(perm_ref[0], mid) + _dot(perm_ref[0], lo)
    h = _rms(x, g_ref[...]).astype(BF16)
    proj = _dot(h, w_ref[...])
    u = proj[:, :W_TOK]
    ub = u.astype(BF16)
    for jt in range(N_BT):
        uk = ub[:, LANE * (jt // 2):LANE * (jt // 2 + 1)]
        sre[:, QB * jt:QB * (jt + 1)] = _dot(uk, bwre_ref[jt])
        sim[:, QB * jt:QB * (jt + 1)] = _dot(uk, bwim_ref[jt])

    if segmented:
        _scan_segments(sre, sim, lam_ref, hout_ref)
    else:
        _scan_rows(sre, sim, lam_ref, hout_ref, tt)

    ys = []
    for jc in range(N_CT):
        ks = slice(jc * 4 * QB, (jc + 1) * 4 * QB)
        ys.append(_dot(sre[:, ks].astype(BF16), cwre_ref[jc]) - _dot(sim[:, ks].astype(BF16), cwim_ref[jc]))
    y = jnp.concatenate(ys, axis=-1) + d_ref[...] * u
    y = jax.nn.gelu(y)
    z = _dot(y.astype(BF16), wglu_ref[...]) + bglu_ref[...]
    mix = (y * (1.0 / (1.0 + jnp.exp(-z)))).astype(BF16)
    qm = (_head_rms(proj[:, W_TOK:], mn_ref[...], bd_ref[...]) * ATTN_SCALE).astype(BF16)
    if segmented:
        mix = _dot(perm_ref[1], mix).astype(BF16)
        qm = _dot(perm_ref[1], qm).astype(BF16)
    mix_ref[0] = mix
    qm_ref[0] = qm


def _s5(x, g, w, bd, mn, bwre, bwim, lam, cwre, cwim, d, wglu, bglu, h0, tt):
    b, t, _ = x.shape
    segmented = tt == S5_SEG * S5_STEPS
    r = np.arange(tt)
    back = (r % S5_STEPS)[:, None] * S5_SEG + (r // S5_STEPS)[:, None] == r[None, :]
    perm = jnp.asarray(np.stack([back.T, back]), BF16)
    tspec = lambda w_: pl.BlockSpec((1, tt, w_), lambda i, j: (i, j, 0))
    sspec = pl.BlockSpec((1, 2, SSM_N), lambda i, j: (i, 0, 0))
    return pl.pallas_call(
        functools.partial(_s5_kernel, segmented),
        grid=(b, t // tt),
        in_specs=[tspec(D_MODEL), _const_spec((1, D_MODEL)), _const_spec((D_MODEL, W_TOK + W_MEM)),
                  _const_spec((W_MEM, W_MEM)), _const_spec((1, W_MEM)),
                  _const_spec((N_BT, LANE, QB)), _const_spec((N_BT, LANE, QB)),
                  _const_spec((2, S5_STEPS, SSM_N)), _const_spec((2, tt, tt)),
                  _const_spec((N_CT, 4 * QB, QB)), _const_spec((N_CT, 4 * QB, QB)),
                  _const_spec((1, W_TOK)), _const_spec((W_TOK, W_TOK)), _const_spec((1, W_TOK)), sspec],
        out_specs=[tspec(W_TOK), tspec(W_MEM), sspec],
        out_shape=[jax.ShapeDtypeStruct((b, t, W_TOK), BF16), jax.ShapeDtypeStruct((b, t, W_MEM), BF16),
                   jax.ShapeDtypeStruct((b, 2, SSM_N), F32)],
        scratch_shapes=[pltpu.VMEM((tt, SSM_N), F32), pltpu.VMEM((tt, SSM_N), F32)],
        compiler_params=_cparams(("parallel", "arbitrary")),
        name="s5_mixer",
    )(x, g, w, bd, mn, bwre, bwim, lam, perm, cwre, cwim, d, wglu, bglu, h0)


def _s5_weights(b_bar_re, b_bar_im, c_re, c_im):
    eye = jnp.eye(SSM_G, dtype=F32)

    def b_tiles(bb):
        full = jnp.einsum("gcp,gh->gchp", bb, eye).reshape(W_TOK, SSM_N)
        return jnp.stack([full[LANE * (j // 2):LANE * (j // 2 + 1), QB * j:QB * (j + 1)]
                          for j in range(N_BT)]).astype(BF16)

    def c_tiles(cc):
        full = jnp.einsum("gcp,gh->gphc", cc, eye).reshape(SSM_N, W_TOK)
        return jnp.stack([full[4 * QB * j:4 * QB * (j + 1), QB * j:QB * (j + 1)]
                          for j in range(N_CT)]).astype(BF16)

    return b_tiles(b_bar_re), b_tiles(b_bar_im), c_tiles(c_re), c_tiles(c_im)


def _mem_kv_kernel(m_ref, g_ref, w_ref, bd_ref, kn_ref, k_ref, v_ref, kb_ref, vb_ref):
    h = _rms(m_ref[0], g_ref[0]).astype(BF16)
    kv = _dot(h, w_ref[0])
    k = _head_rms(kv[:, :W_MEM], kn_ref[0], bd_ref[...]).T
    v = kv[:, W_MEM:].T
    k_ref[0, 0] = k
    v_ref[0, 0] = v
    kb_ref[0, 0] = k.astype(BF16)
    vb_ref[0, 0] = v.astype(BF16)


def _mem_kv(mem, g, w, bd, kn):
    b = mem.shape[0]
    ospec = pl.BlockSpec((1, 1, N_MEM, W_MEM), lambda l, i: (l, i, 0, 0))
    osh = lambda dt: jax.ShapeDtypeStruct((DEPTH, b, N_MEM, W_MEM), dt)
    return pl.pallas_call(
        _mem_kv_kernel,
        grid=(DEPTH, b),
        in_specs=[pl.BlockSpec((1, N_MEM, D_MODEL), lambda l, i: (i, 0, 0)),
                  pl.BlockSpec((1, 1, D_MODEL), lambda l, i: (l, 0, 0)),
                  pl.BlockSpec((1, D_MODEL, 2 * W_MEM), lambda l, i: (l, 0, 0)),
                  _const_spec((W_MEM, W_MEM)),
                  pl.BlockSpec((1, 1, W_MEM), lambda l, i: (l, 0, 0))],
        out_specs=[ospec] * 4,
        out_shape=[osh(F32), osh(F32), osh(BF16), osh(BF16)],
        compiler_params=_cparams(("parallel", "parallel")),
        name="mem_kv",
    )(mem, g, w, bd, kn)


def _mem_attn_kernel(qm_ref, k_ref, v_ref, o_ref):
    tm = qm_ref.shape[1]
    lo, hi = _half_masks(tm)
    outs = []
    for p in range(H_MEM // 2):
        ls = slice(p * LANE, (p + 1) * LANE)
        qp = qm_ref[0, :, ls]
        kp = k_ref[0, ls, :]
        vp = v_ref[0, ls, :]
        halves = []
        for msk in (lo, hi):
            s = _dot(jnp.where(msk, qp, jnp.zeros_like(qp)), kp)
            e = jnp.exp(s - jnp.max(s, axis=-1, keepdims=True))
            o = _dot_t(e.astype(BF16), vp)
            halves.append(o / jnp.sum(e, axis=-1, keepdims=True))
        outs.append(jnp.where(lo, halves[0], halves[1]))
    o_ref[0] = jnp.concatenate(outs, axis=-1).astype(BF16)


def _mem_attn(qm, k, v, tm, l):
    b, t, _ = qm.shape
    kvspec = pl.BlockSpec((None, 1, N_MEM, W_MEM), lambda i, j: (l, i, 0, 0))
    tspec = pl.BlockSpec((1, tm, W_MEM), lambda i, j: (i, j, 0))
    return pl.pallas_call(
        _mem_attn_kernel,
        grid=(b, t // tm),
        in_specs=[tspec, kvspec, kvspec],
        out_specs=tspec,
        out_shape=jax.ShapeDtypeStruct((b, t, W_MEM), BF16),
        compiler_params=_cparams(("parallel", "parallel")),
        name="mem_attn",
    )(qm, k, v)


FF_CHUNK = 1024


def _post_kernel(x_ref, mix_ref, mo_ref, wo_ref, g_ref, wu_ref, wd_ref, o_ref):
    o_ref[0] = x_ref[0] + _dot(mix_ref[0], wo_ref[0, :W_TOK, :]) + _dot(mo_ref[0], wo_ref[0, W_TOK:, :])
    hm = _rms(o_ref[0], g_ref[...]).astype(BF16)
    for f in range(D_FF // FF_CHUNK):
        fs = slice(f * FF_CHUNK, (f + 1) * FF_CHUNK)
        a = jnp.maximum(_dot(hm, wu_ref[0, :, fs]), 0.0)
        o_ref[0] += _dot((a * a).astype(BF16), wd_ref[0, fs, :])


def _layer_spec(shape, l):
    nd = len(shape)
    return pl.BlockSpec((1,) + shape, lambda *_: (l,) + (0,) * nd, pipeline_mode=pl.Buffered(1))


def _post(x, mix, mo, wo, g, wu, wd, tm, l):
    b, t, _ = x.shape
    tspec = lambda w_: pl.BlockSpec((1, tm, w_), lambda i, j: (i, j, 0))
    return pl.pallas_call(
        _post_kernel,
        grid=(b, t // tm),
        in_specs=[tspec(D_MODEL), tspec(W_TOK), tspec(W_MEM), _layer_spec((D_MODEL, D_MODEL), l),
                  _const_spec((1, D_MODEL)), _layer_spec((D_MODEL, D_FF), l), _layer_spec((D_FF, D_MODEL), l)],
        out_specs=tspec(D_MODEL),
        out_shape=jax.ShapeDtypeStruct((b, t, D_MODEL), F32),
        compiler_params=_cparams(("parallel", "parallel")),
        name="out_proj_mlp",
    )(x, mix, mo, wo, g, wu, wd)


def _bucket_table(n):
    d = np.arange(n)
    exact = N_BUCKETS // 2
    nf = np.maximum(d, 1).astype(np.float32)
    far = exact + (np.log(nf / np.float32(exact)) / np.float32(math.log(MAX_DISTANCE / exact))
                   * np.float32(N_BUCKETS - exact)).astype(np.int32)
    return np.where(d < exact, d, np.minimum(far, N_BUCKETS - 1)).astype(np.int32)


def _bias_kernel(rb_ref, bk_ref, o_ref):
    bk = bk_ref[...]
    for h in range(H_A):
        far = rb_ref[N_BUCKETS - 1, h]
        for i in range(bk.shape[0]):
            t = jnp.zeros(bk.shape[1:], F32)
            for b in range(N_BUCKETS - 1):
                t = jnp.where(bk[i] == b, rb_ref[b, h] - far, t)
            o_ref[h, i] = t


def _bias_tiles(rel_bias, buckets):
    return pl.pallas_call(
        _bias_kernel,
        in_specs=[pl.BlockSpec(memory_space=pltpu.SMEM), pl.BlockSpec(memory_space=pltpu.VMEM)],
        out_shape=jax.ShapeDtypeStruct((H_A,) + buckets.shape, F32),
        compiler_params=pltpu.CompilerParams(vmem_limit_bytes=VMEM_LIMIT),
        name="rel_bias_tiles",
    )(rel_bias, buckets)


def _prompt_buckets():
    table = _bucket_table(2 * QB)
    tq = np.arange(QB)[:, None]
    sk = np.arange(QB)[None, :]
    diag = table[np.maximum(tq - sk, 0)]
    off = table[QB + tq - sk]
    return np.stack([off, diag]).astype(np.int32)


def _key_to_float(key):
    bits = jnp.where(key < 0, key ^ jnp.int32(0x7FFFFFFF), key)
    return jnp.where(key < KEY_NEG_INF, NEG_INF, pltpu.bitcast(bits, F32))


def _dsa_prompt_kernel(n_keep, qi_ref, wit_ref, ki2_ref, q_ref, k_ref, v_ref, bias_ref, o_ref,
                       sbuf, mbuf, qh_ref, mx_ref, l_ref, acc_ref):
    j = pl.program_id(1)
    nc = j + 1
    lo, hi = _half_masks(QB)
    zero_b = jnp.zeros((QB, LANE), BF16)
    chunk = lambda c: pl.ds(pl.multiple_of(c * QB, QB), QB)

    qi = qi_ref[0]
    qim = [jnp.where(hi if h % 2 else lo, qi[:, LANE * (h // 2):LANE * (h // 2 + 1)], zero_b)
           for h in range(H_IDX)]
    krow = lax.broadcasted_iota(jnp.int32, (QB, QB), 0)
    qcol = lax.broadcasted_iota(jnp.int32, (QB, QB), 1)

    def scores(c):
        kc = ki2_ref[0, chunk(c), :]
        acc = jnp.zeros((QB, QB), F32)
        for h in range(H_IDX):
            acc = acc + wit_ref[0, h:h + 1, :] * jnp.maximum(_dot_t(kc, qim[h]), 0.0)
        return acc

    def score_chunk(c, _):
        sbuf[chunk(c), :] = scores(c)
        return 0

    lax.fori_loop(0, j, score_chunk, 0)
    sbuf[chunk(j), :] = jnp.where(krow <= qcol, scores(j), NEG_INF)

    def count(pred):
        def body(c, acc):
            hit = jnp.where(pred(sbuf[chunk(c), :], c * QB + krow), 1, 0)
            return acc + jnp.sum(hit.reshape(QB // 8, 8, QB), axis=0)
        acc = lax.fori_loop(0, nc, body, jnp.zeros((8, QB), jnp.int32))
        return jnp.sum(acc, axis=0, keepdims=True)

    def bisect(i, carry):
        t, cnt_t = carry
        cand = t + lax.shift_left(jnp.int32(1), 31 - i)
        cand_f = _key_to_float(cand)
        cnt = count(lambda blk, idx: blk >= cand_f)
        ok = cnt >= n_keep
        return jnp.where(ok, cand, t), jnp.where(ok, cnt, cnt_t)

    t, cnt_t = lax.fori_loop(0, 32, bisect, (jnp.full((1, QB), INT_MIN, jnp.int32),
                                             jnp.full((1, QB), 2 ** 30, jnp.int32)))
    t = _key_to_float(t)

    tied = jnp.logical_and(cnt_t > n_keep, t > NEG_INF)
    any_tied = jnp.max(jnp.where(tied, 1, 0)) > 0

    def tie_cut():
        need = n_keep - count(lambda blk, idx: blk > t)

        def step(i, x):
            cand = x + lax.shift_left(jnp.int32(1), 12 - i)
            cnt = count(lambda blk, idx: jnp.logical_and(blk == t, idx < cand))
            return jnp.where(cnt < need, cand, x)

        return lax.fori_loop(0, 13, step, jnp.zeros((1, QB), jnp.int32))

    jcut = lax.cond(any_tied, tie_cut, lambda: jnp.full((1, QB), 2 ** 30, jnp.int32))

    def mask_chunk(c, _):
        s = sbuf[chunk(c), :]
        idx = c * QB + krow
        m = jnp.where(s > t, 0.0, jnp.where(s == t, jnp.where(idx <= jcut, 0.0, NEG_INF), NEG_INF))
        m = jnp.where(s > NEG_INF, m, NEG_INF)
        mbuf[c] = m.T
        return 0

    lax.fori_loop(0, nc, mask_chunk, 0)

    for p in range(H_A // 2):
        qp = q_ref[0, :, p * LANE:(p + 1) * LANE]
        qh_ref[p, :QB, :] = jnp.where(lo, qp, zero_b)
        qh_ref[p, QB:, :] = jnp.where(hi, qp, zero_b)
    mx_ref[...] = jnp.full(mx_ref.shape, NEG_INF, F32)

    def logits(c, p, near):
        r = _dot_t(qh_ref[p], k_ref[0, chunk(c), p * LANE:(p + 1) * LANE])
        out = []
        for hh in range(2):
            s = r[hh * QB:(hh + 1) * QB] + mbuf[c]
            if near is not None:
                s = s + bias_ref[2 * p + hh, near]
            out.append(s)
        return out

    def attend(c, near):
        for p in range(H_A // 2):
            es, alphas = [], []
            for hh, s in enumerate(logits(c, p, near)):
                h = 2 * p + hh
                m_old = mx_ref[h]
                row_max = jnp.max(jnp.maximum(s[:, :LANE], s[:, LANE:]), axis=-1, keepdims=True)
                m_new = jnp.maximum(m_old, row_max)
                mx_ref[h] = m_new
                m_new = jnp.where(m_new == NEG_INF, 0.0, m_new)
                alpha = jnp.exp(m_old - m_new)
                e0 = jnp.exp(s[:, :LANE] - m_new)
                e1 = jnp.exp(s[:, LANE:] - m_new)
                l_ref[h] = alpha * l_ref[h] + (e0 + e1)
                es.append(jnp.concatenate([e0, e1], axis=-1).astype(BF16))
                alphas.append(alpha)
            r = _dot(jnp.concatenate(es, axis=0), v_ref[0, chunk(c), p * LANE:(p + 1) * LANE])
            acc_ref[2 * p] = alphas[0] * acc_ref[2 * p] + r[:QB]
            acc_ref[2 * p + 1] = alphas[1] * acc_ref[2 * p + 1] + r[QB:]

    l_ref[...] = jnp.zeros(l_ref.shape, F32)
    acc_ref[...] = jnp.zeros(acc_ref.shape, F32)

    def far(c, _):
        attend(c, None)
        return 0

    lax.fori_loop(0, jnp.maximum(j - 1, 0), far, 0)

    @pl.when(j >= 1)
    def _():
        attend(j - 1, 0)

    attend(j, 1)
    for p in range(H_A // 2):
        o = [acc_ref[2 * p + hh] / jnp.sum(l_ref[2 * p + hh], axis=-1, keepdims=True) for hh in range(2)]
        o_ref[0, :, p * LANE:(p + 1) * LANE] = jnp.where(lo, o[0], o[1]).astype(BF16)


def _dsa_prompt(qi, wit, ki2, q, k, v, bias, n_keep):
    b, t, _ = q.shape
    nq = t // QB
    qspec = lambda w_: pl.BlockSpec((1, QB, w_), lambda i, j: (i, j, 0))
    kspec = lambda w_: pl.BlockSpec((1, t, w_), lambda i, j: (i, 0, 0))
    stat = pltpu.VMEM((H_A, QB, LANE), F32)
    return pl.pallas_call(
        functools.partial(_dsa_prompt_kernel, n_keep),
        grid=(b, nq),
        in_specs=[qspec(H_IDX * D_IDX), pl.BlockSpec((1, H_IDX, QB), lambda i, j: (i, 0, j)),
                  kspec(2 * D_IDX), qspec(W_TOK), kspec(W_TOK), kspec(W_TOK),
                  _const_spec((H_A, 2, QB, QB))],
        out_specs=qspec(W_TOK),
        out_shape=jax.ShapeDtypeStruct((b, t, W_TOK), BF16),
        scratch_shapes=[pltpu.VMEM((t, QB), F32), pltpu.VMEM((nq, QB, QB), F32),
                        pltpu.VMEM((H_A // 2, 2 * QB, LANE), BF16), stat, stat, stat],
        compiler_params=_cparams(("parallel", "arbitrary")),
        name="dsa_prompt",
    )(qi, wit, ki2, q, k, v, bias)


def _prep_weights(p):
    seg = np.arange(W_TOK) // HEAD_DIM
    bd = jnp.asarray(seg[:, None] == seg[None, :], BF16)
    row = lambda a: a.reshape(a.shape[0], 1, a.shape[-1])
    tile = lambda a, n: row(jnp.tile(a, (1, n)))
    wa = p["w_in_a"]
    pad = jnp.zeros(wa.shape[:2] + (A_END - A_WI - H_IDX,), wa.dtype)
    wa = jnp.concatenate([wa[..., 0:2304], wa[..., 2304:2816], wa[..., 2888:3144], wa[..., 2816:2880],
                          wa[..., 2880:2888], pad], axis=-1)
    return {
        "bd": bd,
        "norm_mix": row(p["norm_mix"]), "norm_mem": row(p["norm_mem"]), "norm_mlp": row(p["norm_mlp"]),
        "w_in_a": wa.astype(BF16), "w_in_b": p["w_in_b"].astype(BF16), "w_in_c": p["w_in_c"].astype(BF16),
        "w_out": p["w_out"].astype(BF16), "w_up": p["w_up"].astype(BF16), "w_down": p["w_down"].astype(BF16),
        "w_mem_kv": p["w_mem_kv"].astype(BF16),
        "qn_a": tile(p["qn_a"], H_A), "kn_a": tile(p["kn_a"], H_A),
        "qn_mem": tile(p["qn_mem"], H_MEM), "kn_mem": tile(p["kn_mem"], H_MEM),
    }


def _s5_prep(p, li):
    lam, bb_re, bb_im = _s5_disc(p["ssm_a_re"][li], p["ssm_a_im"][li], p["ssm_log_dt"][li],
                                 jnp.swapaxes(p["ssm_b_re"][li], 1, 2), jnp.swapaxes(p["ssm_b_im"][li], 1, 2))
    bwre, bwim, cwre, cwim = _s5_weights(bb_re, bb_im, p["ssm_c_re"][li], p["ssm_c_im"][li])
    return (bwre, bwim, lam.reshape(2, S5_STEPS, SSM_N), cwre, cwim, p["ssm_d"][li].reshape(1, W_TOK),
            p["w_glu"][li].astype(BF16), p["b_glu"][li].reshape(1, W_TOK))


PG = 16


def _sample_buckets(t_new):
    table = _bucket_table(2 * PAGE_SIZE + t_new)
    t = np.arange(t_new)[:, None]
    lane = np.arange(PAGE_SIZE)[None, :]
    return np.stack([table[PAGE_SIZE + t - lane], table[np.maximum(t - lane, 0)]]).astype(np.int32)


def _dsa_scores_kernel(n_pages, pt_ref, qis_ref, wi_ref, kin_ref, *rest):
    pages, s_ref = rest[:PG], rest[PG]
    g = pl.program_id(1)
    t_new = wi_ref.shape[1]
    qis = qis_ref[0]
    wi = wi_ref[0]

    def scores(d):
        acc = jnp.zeros((t_new, PAGE_SIZE), F32)
        for h in range(H_IDX):
            acc = acc + wi[:, h:h + 1] * jnp.maximum(d[h * t_new:(h + 1) * t_new, :], 0.0)
        return acc

    for r in range(PG):
        s_ref[g * PG + r] = scores(_dot(qis, pages[r][0, 0].astype(BF16)))

    @pl.when(g == pl.num_programs(1) - 1)
    def _():
        trow = lax.broadcasted_iota(jnp.int32, (t_new, PAGE_SIZE), 0)
        lane = lax.broadcasted_iota(jnp.int32, (t_new, PAGE_SIZE), 1)
        s_ref[n_pages] = jnp.where(lane <= trow, scores(_dot_t(qis, kin_ref[0])), NEG_INF)


ROW_GROUP = 64


def _dsa_threshold_kernel(n_keep, s_ref, m_ref):
    nb, rows, _ = s_ref.shape
    rgs = min(ROW_GROUP, rows)
    lane = lax.broadcasted_iota(jnp.int32, (rgs, PAGE_SIZE), 1)

    for rg in range(rows // rgs):
        rs = slice(rg * rgs, (rg + 1) * rgs)

        def count(pred, rs=rs):
            acc = jnp.zeros((rgs, PAGE_SIZE), jnp.int32)
            for blk in range(nb):
                acc = acc + jnp.where(pred(s_ref[blk, rs, :], blk * PAGE_SIZE + lane), 1, 0)
            return jnp.sum(acc, axis=-1, keepdims=True)

        def bisect(i, carry, count=count):
            t, cnt_t = carry
            cand = t + lax.shift_left(jnp.int32(1), 31 - i)
            cand_f = _key_to_float(cand)
            cnt = count(lambda blk, idx: blk >= cand_f)
            ok = cnt >= n_keep
            return jnp.where(ok, cand, t), jnp.where(ok, cnt, cnt_t)

        t, cnt_t = lax.fori_loop(0, 32, bisect, (jnp.full((rgs, 1), INT_MIN, jnp.int32),
                                                 jnp.full((rgs, 1), 2 ** 30, jnp.int32)))
        t = _key_to_float(t)
        tied = jnp.logical_and(cnt_t > n_keep, t > NEG_INF)
        any_tied = jnp.max(jnp.where(tied, 1, 0)) > 0

        def tie_cut(t=t, count=count):
            need = n_keep - count(lambda blk, idx: blk > t)

            def step(i, x):
                cand = x + lax.shift_left(jnp.int32(1), 14 - i)
                cnt = count(lambda blk, idx: jnp.logical_and(blk == t, idx < cand))
                return jnp.where(cnt < need, cand, x)

            return lax.fori_loop(0, 15, step, jnp.zeros((rgs, 1), jnp.int32))

        jcut = lax.cond(any_tied, tie_cut, lambda: jnp.full((rgs, 1), 2 ** 30, jnp.int32))
        for blk in range(nb):
            s = s_ref[blk, rs, :]
            idx = blk * PAGE_SIZE + lane
            m = jnp.where(s > t, 0.0, jnp.where(s == t, jnp.where(idx <= jcut, 0.0, NEG_INF), NEG_INF))
            m_ref[blk, rs, :] = jnp.where(s > NEG_INF, m, NEG_INF)


def _dsa_threshold(scores, n_keep):
    return pl.pallas_call(
        functools.partial(_dsa_threshold_kernel, n_keep),
        out_shape=jax.ShapeDtypeStruct(scores.shape, F32),
        compiler_params=pltpu.CompilerParams(vmem_limit_bytes=VMEM_LIMIT),
        name="dsa_sample_threshold",
    )(scores)


def _dsa_scores(page_table, qis, wi, ki_new, pool_ki, li):
    bd_, n_pages = page_table.shape
    t_new = wi.shape[1]
    page_spec = lambda r: pl.BlockSpec((1, 1, D_IDX, PAGE_SIZE),
                                       lambda b, g, pt, r=r: (li, pt[b, g * PG + r], 0, 0))
    grid_spec = pltpu.PrefetchScalarGridSpec(
        num_scalar_prefetch=1,
        grid=(bd_, n_pages // PG),
        in_specs=[pl.BlockSpec((1, H_IDX * t_new, D_IDX), lambda b, g, pt: (b, 0, 0)),
                  pl.BlockSpec((1, t_new, H_IDX), lambda b, g, pt: (b, 0, 0)),
                  pl.BlockSpec((1, PAGE_SIZE, D_IDX), lambda b, g, pt: (b, 0, 0))]
                 + [page_spec(r) for r in range(PG)],
        out_specs=pl.BlockSpec((n_pages + 1, t_new, PAGE_SIZE), lambda b, g, pt: (0, b, 0)),
    )
    return pl.pallas_call(
        functools.partial(_dsa_scores_kernel, n_pages),
        grid_spec=grid_spec,
        out_shape=jax.ShapeDtypeStruct((n_pages + 1, bd_ * t_new, PAGE_SIZE), F32),
        compiler_params=_cparams(("parallel", "arbitrary")),
        name="dsa_sample_scores",
    )(page_table, qis, wi, ki_new, *([pool_ki] * PG))


def _dsa_attend_kernel(n_pages, pt_ref, qbd_ref, mask_ref, bias_ref, kn_ref, vn_ref, *rest):
    kpages, vpages = rest[:PG], rest[PG:2 * PG]
    o_ref, mx_ref, l_ref, acc_ref, kstage, vstage = rest[2 * PG:]
    g = pl.program_id(1)
    ng = pl.num_programs(1)
    t_new = mask_ref.shape[1]
    rows = H_A * t_new
    qbd = qbd_ref[0]

    def head_tile(m):
        return jnp.tile(m, (H_A, 1))

    def update(parts, pv):
        m_old = mx_ref[...]
        cm = parts[0]
        for part in parts[1:]:
            cm = jnp.maximum(cm, part)
        m_new = jnp.maximum(m_old, jnp.max(cm, axis=-1, keepdims=True))
        mx_ref[...] = m_new
        m_new = jnp.where(m_new == NEG_INF, 0.0, m_new)
        alpha = jnp.exp(m_old - m_new)
        es = [jnp.exp(part - m_new) for part in parts]
        lsum = alpha * l_ref[...]
        for e in es:
            lsum = lsum + e
        l_ref[...] = lsum
        acc_ref[...] = (jnp.tile(alpha, (1, W_TOK // LANE)) * acc_ref[...]
                        + pv(jnp.concatenate(es, axis=-1).astype(BF16)))

    @pl.when(g == 0)
    def _():
        mx_ref[...] = jnp.full((rows, LANE), NEG_INF, F32)
        l_ref[...] = jnp.zeros((rows, LANE), F32)
        acc_ref[...] = jnp.zeros((rows, W_TOK), F32)
        sn = _dot_t(qbd, kn_ref[0]) + head_tile(mask_ref[n_pages]) + bias_ref[1]
        update([sn], lambda e: _dot(e, vn_ref[0]))

    for r in range(PG):
        kstage[:, r * PAGE_SIZE:(r + 1) * PAGE_SIZE] = kpages[r][0, 0].astype(BF16)
        vstage[:, r * PAGE_SIZE:(r + 1) * PAGE_SIZE] = vpages[r][0, 0].astype(BF16)
    s = _dot(qbd, kstage[...])
    parts = []
    for r in range(PG):
        part = s[:, r * LANE:(r + 1) * LANE] + head_tile(mask_ref[g * PG + r])
        if r == PG - 1:
            part = part + jnp.where(g == ng - 1, 1.0, 0.0) * bias_ref[0]
        parts.append(part)
    update(parts, lambda e: _dot_t(e, vstage[...]))

    @pl.when(g == ng - 1)
    def _():
        o = acc_ref[...] / jnp.sum(l_ref[...], axis=-1, keepdims=True)
        lo, _ = _half_masks(t_new)
        outs = []
        for p in range(H_A // 2):
            ls = slice(p * LANE, (p + 1) * LANE)
            outs.append(jnp.where(lo, o[2 * p * t_new:(2 * p + 1) * t_new, ls],
                                  o[(2 * p + 1) * t_new:(2 * p + 2) * t_new, ls]))
        o_ref[0] = jnp.concatenate(outs, axis=-1).astype(BF16)


def _dsa_attend(page_table, qbd, mask, bias, k_new, v_new, pool_k, pool_v, li):
    bd_, n_pages = page_table.shape
    rows = qbd.shape[1]
    t_new = rows // H_A
    ng = n_pages // PG
    pspec = lambda r: pl.BlockSpec((1, 1, W_TOK, PAGE_SIZE), lambda b, g, pt, r=r: (li, pt[b, g * PG + r], 0, 0))
    per_seq = lambda shape: pl.BlockSpec((1,) + shape, lambda b, g, pt: (b,) + (0,) * len(shape))
    grid_spec = pltpu.PrefetchScalarGridSpec(
        num_scalar_prefetch=1,
        grid=(bd_, ng),
        in_specs=[per_seq((rows, W_TOK)),
                  pl.BlockSpec((n_pages + 1, t_new, PAGE_SIZE), lambda b, g, pt: (0, b, 0)),
                  pl.BlockSpec((2, rows, PAGE_SIZE), lambda b, g, pt: (0, 0, 0)),
                  per_seq((PAGE_SIZE, W_TOK)), per_seq((PAGE_SIZE, W_TOK))]
                 + [pspec(r) for r in range(PG)] * 2,
        out_specs=per_seq((t_new, W_TOK)),
        scratch_shapes=[pltpu.VMEM((rows, LANE), F32), pltpu.VMEM((rows, LANE), F32),
                        pltpu.VMEM((rows, W_TOK), F32), pltpu.VMEM((W_TOK, PG * PAGE_SIZE), BF16),
                        pltpu.VMEM((W_TOK, PG * PAGE_SIZE), BF16)],
    )
    return pl.pallas_call(
        functools.partial(_dsa_attend_kernel, n_pages),
        grid_spec=grid_spec,
        out_shape=jax.ShapeDtypeStruct((bd_, t_new, W_TOK), BF16),
        compiler_params=_cparams(("parallel", "arbitrary")),
        name="dsa_sample_attend",
    )(page_table, qbd, mask, bias, k_new, v_new, *([pool_k] * PG), *([pool_v] * PG))


def _dsa_sample(outs, bias_s, pool_k, pool_v, pool_ki, li, page_table):
    kf, vf, kif, qb, kb, vb, qib, ki2, wis, qmb = outs
    bd_, n_pages = page_table.shape
    t_new = kf.shape[1] // bd_
    n_keep = min(TOPK_MAX, (n_pages * PAGE_SIZE + t_new) // 4)
    qis = qib.reshape(bd_, t_new, H_IDX, D_IDX).transpose(0, 2, 1, 3).reshape(bd_, H_IDX * t_new, D_IDX)
    pad_rows = lambda a: jnp.pad(a.reshape(bd_, t_new, a.shape[-1]), ((0, 0), (0, PAGE_SIZE - t_new), (0, 0)))
    scores = _dsa_scores(page_table, qis, wis.reshape(bd_, t_new, H_IDX), pad_rows(ki2[..., :D_IDX]),
                         jnp.swapaxes(pool_ki, 2, 3), li)
    mask = _dsa_threshold(scores, n_keep)
    q4 = qb.reshape(bd_, t_new, H_A, HEAD_DIM)
    qbd = jnp.einsum("bthd,hg->bhtgd", q4, jnp.eye(H_A, dtype=q4.dtype)).reshape(bd_, H_A * t_new, W_TOK)
    page_t = lambda a: jnp.transpose(a, (0, 1, 3, 4, 2)).reshape(a.shape[:2] + (W_TOK, PAGE_SIZE))
    return _dsa_attend(page_table, qbd, mask, bias_s, pad_rows(kb), pad_rows(vb), page_t(pool_k), page_t(pool_v), li)


def _trunk(x, p, pw, s5w, mem_k, mem_v, attend, conv_state, ssm_state, tm, flat):
    b, t, _ = x.shape
    shp = (1, b * t) if flat else (b, t)
    fl = lambda a: a.reshape(shp + a.shape[2:])
    unfl = lambda a: a.reshape((b, t) + a.shape[2:])
    bdm = pw["bd"][:W_MEM, :W_MEM]
    ks, vs, kis, convs, res, ims = [], [], [], [], [], []
    for l in range(DEPTH):
        kind, li = l % N_MIXERS, l // N_MIXERS
        if kind == 0:
            outs = _in_a(fl(x), pw["norm_mix"][l], pw["w_in_a"][li], pw["bd"], pw["qn_a"][li], pw["kn_a"][li],
                         pw["qn_mem"][l], tm)
            mix = unfl(attend(li, outs))
            ks.append(outs[0].reshape(b, t, H_A, HEAD_DIM))
            vs.append(outs[1].reshape(b, t, H_A, HEAD_DIM))
            kis.append(unfl(outs[2]))
            qm = unfl(outs[9])
        elif kind == 1:
            mix, qm, nst = _in_b(x, pw["norm_mix"][l], pw["w_in_b"][li], bdm, pw["qn_mem"][l], p["conv_w"][li],
                                 conv_state(li), min(tm, t))
            convs.append(nst)
        else:
            mix, qm, hout = _s5(x, pw["norm_mix"][l], pw["w_in_c"][li], bdm, pw["qn_mem"][l], *s5w[li],
                                ssm_state(li), min(tm // 2, t))
            res.append(hout[:, 0].reshape(b, SSM_G, SSM_P))
            ims.append(hout[:, 1].reshape(b, SSM_G, SSM_P))
        mo = _mem_attn(qm, mem_k, mem_v, min(tm, t), l)
        x = unfl(_post(fl(x), fl(mix), fl(mo), pw["w_out"], pw["norm_mlp"][l], pw["w_up"], pw["w_down"], tm, l))
    return x, jnp.stack(ks), jnp.stack(vs), jnp.stack(kis), jnp.stack(convs), jnp.stack(res), jnp.stack(ims)


def kernel(x_prompt, x_sample, cache_k, cache_v, cache_kidx, state_conv, state_ssm_re, state_ssm_im,
           cache_mem_k, cache_mem_v, page_table, mem_prompt, norm_mix, norm_mem, norm_mlp,
           w_in_a, w_in_b, w_in_c, w_out, qn_a, kn_a, rel_bias, conv_w, ssm_a_re, ssm_a_im, ssm_log_dt,
           ssm_b_re, ssm_b_im, ssm_c_re, ssm_c_im, ssm_d, w_glu, b_glu, w_mem_kv, qn_mem, kn_mem, w_up, w_down):
    p = dict(norm_mix=norm_mix, norm_mem=norm_mem, norm_mlp=norm_mlp, w_in_a=w_in_a, w_in_b=w_in_b,
             w_in_c=w_in_c, w_out=w_out, qn_a=qn_a, kn_a=kn_a, conv_w=conv_w, ssm_a_re=ssm_a_re,
             ssm_a_im=ssm_a_im, ssm_log_dt=ssm_log_dt, ssm_b_re=ssm_b_re, ssm_b_im=ssm_b_im, ssm_c_re=ssm_c_re,
             ssm_c_im=ssm_c_im, ssm_d=ssm_d, w_glu=w_glu, b_glu=b_glu, w_mem_kv=w_mem_kv, qn_mem=qn_mem,
             kn_mem=kn_mem, w_up=w_up, w_down=w_down)
    pw = _prep_weights(p)
    n_c = ssm_a_re.shape[0]
    s5w = [_s5_prep(p, li) for li in range(n_c)]
    bp, sp, _ = x_prompt.shape
    bd_, td, _ = x_sample.shape

    bias_p = _bias_tiles(rel_bias, jnp.asarray(_prompt_buckets()))
    pmk, pmv, pmk_b, pmv_b = _mem_kv(mem_prompt, pw["norm_mem"], pw["w_mem_kv"], pw["bd"][:W_MEM, :W_MEM],
                                     pw["kn_mem"])

    def attend_prompt(li, outs):
        kf, vf, kif, qb, kb, vb, qib, ki2, wis, qmb = outs
        return _dsa_prompt(qib, jnp.swapaxes(wis, 1, 2), ki2, qb, kb, vb, bias_p, min(TOPK_MAX, sp // 4))

    y_prompt, pk, pv, pki, pconv, pre, pim = _trunk(
        x_prompt, p, pw, s5w, pmk_b, pmv_b, attend_prompt,
        lambda li: jnp.zeros((bp, CONV_W - 1, W_TOK), F32),
        lambda li: jnp.zeros((bp, 2, SSM_N), F32), 512, False)

    bias_s = _bias_tiles(rel_bias, jnp.asarray(_sample_buckets(td)))
    bias_s = jnp.swapaxes(bias_s, 0, 1).reshape(2, H_A * td, PAGE_SIZE)
    mem_t = lambda a: jnp.transpose(a, (0, 1, 3, 4, 2)).reshape(DEPTH, bd_, W_MEM, N_MEM).astype(BF16)
    smk = mem_t(cache_mem_k)
    smv = mem_t(cache_mem_v)

    def attend_sample(li, outs):
        return _dsa_sample(outs, bias_s, cache_k, cache_v, cache_kidx, li, page_table)

    y_sample, sk, sv, ski, sconv, sre, sim = _trunk(
        x_sample, p, pw, s5w, smk, smv, attend_sample,
        lambda li: state_conv[li],
        lambda li: jnp.stack([state_ssm_re[li].reshape(bd_, SSM_N), state_ssm_im[li].reshape(bd_, SSM_N)], axis=1),
        bd_ * td, True)

    mem_out = lambda a: jnp.transpose(a.reshape(DEPTH, bp, H_MEM, HEAD_DIM, N_MEM), (0, 1, 4, 2, 3))
    return (y_prompt, y_sample, pk, pv, pki, pconv, pre, pim, mem_out(pmk), mem_out(pmv),
            sk, sv, ski, sconv, sre, sim)
```
